```python
import jax, jax.numpy as jnp
from jax import lax
import numpy as np

D_MODEL = 1024
BATCH = 16
SEQ = 2048
DEPTH = 1

PLE_DIM = 256
CONV_WIDTH = D_MODEL // 2
CONV_GROUPS = 8
CONV_K = 3
RET_HEADS = 4
RET_HEAD_DIM = 128
RET_WIDTH = RET_HEADS * RET_HEAD_DIM
MIX_WIDTH = CONV_WIDTH + RET_WIDTH
IN_COLS = 3 * CONV_WIDTH + 4 * RET_WIDTH
D_FF = -(-8 * D_MODEL // (3 * 256)) * 256
CHUNK = 128
ROPE_BASE = 10000.0
EPS = 1e-6

kernel_name = "hybrid_shortconv_retention_block"


def rmsnorm(x, g):
    xf = x.astype(jnp.float32)
    y = xf * lax.rsqrt(jnp.mean(xf * xf, axis=-1, keepdims=True) + EPS)
    return (y * g.astype(jnp.float32)).astype(x.dtype)


def head_groupnorm(y, g):
    mu = jnp.mean(y, axis=-1, keepdims=True)
    var = jnp.mean(jnp.square(y - mu), axis=-1, keepdims=True)
    yn = (y - mu) * lax.rsqrt(var + EPS)
    b, s, h, d = y.shape
    return yn.reshape(b, s, h * d) * g.astype(jnp.float32)


def rope(x, pos):
    half = x.shape[-1] // 2
    inv_freq = ROPE_BASE ** (-jnp.arange(half, dtype=jnp.float32) / half)
    ang = pos[:, None] * inv_freq[None, :]
    cos = jnp.cos(ang)[None, :, None, :]
    sin = jnp.sin(ang)[None, :, None, :]
    x1, x2 = x[..., :half], x[..., half:]
    return jnp.concatenate([x1 * cos - x2 * sin, x2 * cos + x1 * sin], axis=-1)


def retention_chunkwise(q, k, v):
    b, s, h, d = q.shape
    n = s // CHUNK
    log_gamma = jnp.log(1.0 - jnp.power(2.0, -5.0 - jnp.arange(h, dtype=jnp.float32)))

    def to_chunks(t):
        return t.reshape(b, n, CHUNK, h, d).transpose(0, 3, 1, 2, 4)

    qc, kc, vc = to_chunks(q), to_chunks(k), to_chunks(v)
    idx = jnp.arange(CHUNK, dtype=jnp.float32)
    diff = idx[:, None] - idx[None, :]
    decay_mask = jnp.where(diff[None] >= 0,
                           jnp.exp(log_gamma[:, None, None] * jnp.maximum(diff, 0.0)[None]),
                           0.0)

    scores = jnp.einsum('bhncd,bhnmd->bhncm', qc, kc) * decay_mask[None, :, None]
    intra = jnp.einsum('bhncm,bhnme->bhnce', scores, vc)

    k_decay = jnp.exp(log_gamma[:, None] * (CHUNK - 1 - idx)[None])
    kv = jnp.einsum('bhncd,hc,bhnce->bhnde', kc, k_decay, vc)
    chunk_decay = jnp.exp(log_gamma * CHUNK)[None, :, None, None]

    def step(state, kv_c):
        return state * chunk_decay + kv_c, state

    init = jnp.zeros((b, h, d, d), jnp.float32)
    _, states_in = lax.scan(step, init, jnp.moveaxis(kv, 2, 0))
    states_in = jnp.moveaxis(states_in, 0, 2)

    q_decay = jnp.exp(log_gamma[:, None] * (idx + 1.0)[None])
    cross = jnp.einsum('bhncd,hc,bhnde->bhnce', qc, q_decay, states_in)

    out = intra + cross
    return out.transpose(0, 2, 3, 1, 4).reshape(b, s, h, d)


def setup_inputs(seed: int = 0) -> dict:
    key = jax.random.key(seed)
    ks = jax.random.split(key, 16)
    f32 = jnp.float32

    def w(k, shape, fan_in):
        return jax.random.normal(k, shape, f32) * (fan_in ** -0.5)

    def gain(k, shape):
        return 1.0 + 0.02 * jax.random.normal(k, shape, f32)

    return {
        "x": jax.random.normal(ks[0], (BATCH, SEQ, D_MODEL), f32),
        "p": jax.random.normal(ks[1], (DEPTH, BATCH, SEQ, PLE_DIM), f32),
        "g_mix": gain(ks[2], (DEPTH, D_MODEL)),
        "w_in": w(ks[3], (DEPTH, D_MODEL, IN_COLS), D_MODEL),
        "conv_w": w(ks[4], (DEPTH, CONV_K, CONV_WIDTH), CONV_K),
        "ret_gn": gain(ks[5], (DEPTH, RET_WIDTH)),
        "w_out": w(ks[6], (DEPTH, MIX_WIDTH, D_MODEL), MIX_WIDTH),
        "g_ffn": gain(ks[7], (DEPTH, D_MODEL)),
        "w_gate": w(ks[8], (DEPTH, D_MODEL, D_FF), D_MODEL),
        "w_up": w(ks[9], (DEPTH, D_MODEL, D_FF), D_MODEL),
        "w_down": w(ks[10], (DEPTH, D_FF, D_MODEL), D_FF),
        "g_ple": gain(ks[11], (DEPTH, D_MODEL)),
        "w_ple_gate": w(ks[12], (DEPTH, D_MODEL, D_MODEL), D_MODEL),
        "w_ple_proj": w(ks[13], (DEPTH, PLE_DIM, D_MODEL), PLE_DIM),
        "g_final": gain(ks[14], (D_MODEL,)),
    }


def reference(x, p, g_mix, w_in, conv_w, ret_gn, w_out, g_ffn, w_gate, w_up,
              w_down, g_ple, w_ple_gate, w_ple_proj, g_final):
    b, s, _ = x.shape
    pos = jnp.arange(s, dtype=jnp.float32)
    splits = np.cumsum([CONV_WIDTH] * 3 + [RET_WIDTH] * 3).tolist()
    h = x
    for i in range(DEPTH):
        u = rmsnorm(h, g_mix[i])
        proj = u @ w_in[i]
        cb, cc, cx, rq, rk, rv, rg = jnp.split(proj, splits, axis=-1)

        z = cc * cx
        z_pad = jnp.pad(z, ((0, 0), (CONV_K - 1, 0), (0, 0)))
        conv = sum(conv_w[i][j] * z_pad[:, j:j + s] for j in range(CONV_K))
        y_conv = cb * conv

        q = rope(rq.reshape(b, s, RET_HEADS, RET_HEAD_DIM).astype(jnp.float32), pos)
        k = rope(rk.reshape(b, s, RET_HEADS, RET_HEAD_DIM).astype(jnp.float32), pos)
        k = k * (RET_HEAD_DIM ** -0.5)
        v = rv.reshape(b, s, RET_HEADS, RET_HEAD_DIM).astype(jnp.float32)
        y_ret = head_groupnorm(retention_chunkwise(q, k, v), ret_gn[i])
        y_ret = (jax.nn.silu(rg.astype(jnp.float32)) * y_ret).astype(x.dtype)

        h = h + jnp.concatenate([y_conv, y_ret], axis=-1) @ w_out[i]

        u = rmsnorm(h, g_ffn[i])
        h = h + (jax.nn.silu(u @ w_gate[i]) * (u @ w_up[i])) @ w_down[i]

        u = rmsnorm(h, g_ple[i])
        h = h + jax.nn.sigmoid(u @ w_ple_gate[i]) * (p[i] @ w_ple_proj[i])
    return rmsnorm(h, g_final)
```

```python
import functools

import jax
import jax.numpy as jnp
from jax import lax
from jax.experimental import pallas as pl
from jax.experimental.pallas import tpu as pltpu

D_MODEL = 1024
PLE_DIM = 256
CONV_WIDTH = 512
CONV_K = 3
RET_HEADS = 4
RET_HEAD_DIM = 128
RET_WIDTH = RET_HEADS * RET_HEAD_DIM
MIX_WIDTH = CONV_WIDTH + RET_WIDTH
CONV_COLS = 3 * CONV_WIDTH
IN_COLS = CONV_COLS + 4 * RET_WIDTH
D_FF = 2816
ROPE_BASE = 10000.0
EPS = 1e-6

SEQ_TILE = 256
HIST_ROWS = 8
VMEM_LIMIT_BYTES = 58 * 1024 * 1024

F32 = jnp.float32
BF16 = jnp.bfloat16


def _rmsnorm(x, g):
    return x * lax.rsqrt(jnp.mean(x * x, axis=-1, keepdims=True) + EPS) * g


def _silu(x):
    return x * jax.nn.sigmoid(x)


def _dot(a, b):
    return jnp.dot(a, b, preferred_element_type=F32)


def _block_kernel(x_ref, p_ref, cq_ref, sq_ref, ck_ref, sk_ref,
                  mask_ref, qd_ref, kd_ref, cd_ref,
                  gmix_ref, win_ref, convw_ref, gn_ref, wout_ref,
                  gffn_ref, wg_ref, wu_ref, wd_ref,
                  gple_ref, wpg_ref, wpp_ref, gfin_ref,
                  o_ref, state_ref, zbuf_ref, y_ref):
    ts = SEQ_TILE
    s = pl.program_id(1)

    @pl.when(s == 0)
    def _():
        state_ref[...] = jnp.zeros_like(state_ref)
        zbuf_ref[0:HIST_ROWS, :] = jnp.zeros((HIST_ROWS, CONV_WIDTH), F32)

    @pl.when(s > 0)
    def _():
        zbuf_ref[0:HIST_ROWS, :] = zbuf_ref[ts:ts + HIST_ROWS, :]

    x = x_ref[0]
    u = _rmsnorm(x, gmix_ref[...]).astype(BF16)

    pc = _dot(u, win_ref[:, 0:CONV_COLS])
    cb = pc[:, 0:CONV_WIDTH]
    z = pc[:, CONV_WIDTH:2 * CONV_WIDTH] * pc[:, 2 * CONV_WIDTH:3 * CONV_WIDTH]
    zbuf_ref[HIST_ROWS:HIST_ROWS + ts, :] = z
    z1 = zbuf_ref[HIST_ROWS - 1:HIST_ROWS - 1 + ts, :]
    z2 = zbuf_ref[HIST_ROWS - 2:HIST_ROWS - 2 + ts, :]
    conv = convw_ref[0:1, :] * z2 + convw_ref[1:2, :] * z1 + convw_ref[2:3, :] * z
    y_ref[:, 0:CONV_WIDTH] = (cb * conv).astype(BF16)

    pr = _dot(u, win_ref[:, CONV_COLS:IN_COLS])
    cq, sq, ck, sk = cq_ref[...], sq_ref[...], ck_ref[...], sk_ref[...]
    half = RET_HEAD_DIM // 2
    for h in range(RET_HEADS):
        lo = h * RET_HEAD_DIM
        hi = lo + RET_HEAD_DIM
        q = pr[:, lo:hi]
        k = pr[:, RET_WIDTH + lo:RET_WIDTH + hi]
        v = pr[:, 2 * RET_WIDTH + lo:2 * RET_WIDTH + hi]
        g = pr[:, 3 * RET_WIDTH + lo:3 * RET_WIDTH + hi]
        qr = q * cq + pltpu.roll(q, half, 1) * sq
        kr = k * ck + pltpu.roll(k, half, 1) * sk
        qb = qr.astype(BF16)
        kb = kr.astype(BF16)
        vb = v.astype(BF16)
        scores = lax.dot_general(qb, kb, (((1,), (1,)), ((), ())),
                                 preferred_element_type=F32)
        intra = _dot((scores * mask_ref[h]).astype(BF16), vb)
        st = state_ref[h]
        cross = _dot((qr * qd_ref[h]).astype(BF16), st.astype(BF16))
        o = intra + cross
        kv = lax.dot_general((kr * kd_ref[h]).astype(BF16), vb,
                             (((0,), (0,)), ((), ())), preferred_element_type=F32)
        state_ref[h] = st * cd_ref[h] + kv
        mu = jnp.mean(o, axis=-1, keepdims=True)
        d = o - mu
        var = jnp.mean(d * d, axis=-1, keepdims=True)
        yn = d * lax.rsqrt(var + EPS) * gn_ref[:, lo:hi]
        y_ref[:, CONV_WIDTH + lo:CONV_WIDTH + hi] = (_silu(g) * yn).astype(BF16)

    h1 = x + _dot(y_ref[...], wout_ref[...])

    u2 = _rmsnorm(h1, gffn_ref[...]).astype(BF16)
    hid = (_silu(_dot(u2, wg_ref[...])) * _dot(u2, wu_ref[...])).astype(BF16)
    h2 = h1 + _dot(hid, wd_ref[...])

    u3 = _rmsnorm(h2, gple_ref[...]).astype(BF16)
    gate = jax.nn.sigmoid(_dot(u3, wpg_ref[...]))
    h3 = h2 + gate * _dot(p_ref[0].astype(BF16), wpp_ref[...])

    o_ref[0] = _rmsnorm(h3, gfin_ref[...])


def _rope_tables(seq, k_scale):
    half = RET_HEAD_DIM // 2
    pos = jnp.arange(seq, dtype=F32)
    inv_freq = ROPE_BASE ** (-jnp.arange(half, dtype=F32) / half)
    ang = pos[:, None] * inv_freq[None, :]
    cos = jnp.cos(ang)
    sin = jnp.sin(ang)
    cos_full = jnp.concatenate([cos, cos], axis=-1)
    sin_full = jnp.concatenate([-sin, sin], axis=-1)
    return cos_full, sin_full, cos_full * k_scale, sin_full * k_scale


def _decay_tables(chunk):
    hh = jnp.arange(RET_HEADS, dtype=F32)
    log_gamma = jnp.log(1.0 - jnp.power(2.0, -5.0 - hh))
    idx = jnp.arange(chunk, dtype=F32)
    diff = idx[:, None] - idx[None, :]
    mask = jnp.where(diff[None] >= 0,
                     jnp.exp(log_gamma[:, None, None] * jnp.maximum(diff, 0.0)[None]),
                     0.0)
    q_decay = jnp.exp(log_gamma[:, None] * (idx + 1.0)[None])
    k_decay = jnp.exp(log_gamma[:, None] * (chunk - 1 - idx)[None])
    chunk_decay = jnp.exp(log_gamma * chunk)
    lanes = (RET_HEADS, chunk, RET_HEAD_DIM)
    qd = jnp.broadcast_to(q_decay[:, :, None], lanes)
    kd = jnp.broadcast_to(k_decay[:, :, None], lanes)
    cd = jnp.broadcast_to(chunk_decay[:, None, None], (RET_HEADS, 1, RET_HEAD_DIM))
    return mask, qd, kd, cd


def kernel(x, p, g_mix, w_in, conv_w, ret_gn, w_out, g_ffn, w_gate, w_up, w_down,
           g_ple, w_ple_gate, w_ple_proj, g_final):
    b, s, d = x.shape
    assert d == D_MODEL and s % SEQ_TILE == 0
    assert p.shape[0] == 1, "single-layer block"
    ts = SEQ_TILE

    cq, sq, ck, sk = _rope_tables(s, RET_HEAD_DIM ** -0.5)
    mask, qd, kd, cd = _decay_tables(ts)

    row = lambda a: a.reshape(1, -1).astype(F32)
    whole = pl.BlockSpec(memory_space=pltpu.VMEM)
    rope_spec = pl.BlockSpec((ts, RET_HEAD_DIM), lambda bi, si: (si, 0))

    operands = [
        (x, pl.BlockSpec((1, ts, D_MODEL), lambda bi, si: (bi, si, 0))),
        (p[0], pl.BlockSpec((1, ts, PLE_DIM), lambda bi, si: (bi, si, 0))),
        (cq, rope_spec), (sq, rope_spec), (ck, rope_spec), (sk, rope_spec),
        (mask, whole), (qd, whole), (kd, whole), (cd, whole),
        (row(g_mix[0]), whole), (w_in[0].astype(BF16), whole),
        (conv_w[0].astype(F32), whole), (row(ret_gn[0]), whole),
        (w_out[0].astype(BF16), whole),
        (row(g_ffn[0]), whole), (w_gate[0].astype(BF16), whole),
        (w_up[0].astype(BF16), whole), (w_down[0].astype(BF16), whole),
        (row(g_ple[0]), whole), (w_ple_gate[0].astype(BF16), whole),
        (w_ple_proj[0].astype(BF16), whole), (row(g_final), whole),
    ]
    args = [a for a, _ in operands]
    in_specs = [sp for _, sp in operands]

    return pl.pallas_call(
        _block_kernel,
        grid=(b, s // ts),
        in_specs=in_specs,
        out_specs=pl.BlockSpec((1, ts, D_MODEL), lambda bi, si: (bi, si, 0)),
        out_shape=jax.ShapeDtypeStruct((b, s, D_MODEL), x.dtype),
        scratch_shapes=[
            pltpu.VMEM((RET_HEADS, RET_HEAD_DIM, RET_HEAD_DIM), F32),
            pltpu.VMEM((HIST_ROWS + ts, CONV_WIDTH), F32),
            pltpu.VMEM((ts, MIX_WIDTH), BF16),
        ],
        compiler_params=pltpu.CompilerParams(
            dimension_semantics=("arbitrary", "arbitrary"),
            vmem_limit_bytes=VMEM_LIMIT_BYTES),
        name="hybrid_block",
    )(*args)
```

```python
import jax
import jax.numpy as jnp
from jax import lax
from jax.experimental import pallas as pl
from jax.experimental.pallas import tpu as pltpu

D_MODEL = 1024
PLE_DIM = 256
CONV_WIDTH = 512
CONV_K = 3
RET_HEADS = 4
RET_HEAD_DIM = 128
RET_WIDTH = RET_HEADS * RET_HEAD_DIM
MIX_WIDTH = CONV_WIDTH + RET_WIDTH
CONV_COLS = 3 * CONV_WIDTH
IN_COLS = CONV_COLS + 4 * RET_WIDTH
D_FF = 2816
ROPE_BASE = 10000.0
EPS = 1e-6

SEQ_TILE = 512
RET_CHUNK = 256
HIST_ROWS = 8
VMEM_LIMIT_BYTES = 60 * 1024 * 1024

F32 = jnp.float32
BF16 = jnp.bfloat16


def _rmsnorm(x, g):
    return x * lax.rsqrt(jnp.mean(x * x, axis=-1, keepdims=True) + EPS) * g


def _silu(x):
    return x * jax.nn.sigmoid(x)


def _block_kernel(x_ref, p_ref, cq_ref, sq_ref, ck_ref, sk_ref,
                  mask_ref, qd_ref, kd_ref, cd_ref,
                  gmix_ref, win_ref, convw_ref, gn_ref, wout_ref,
                  gffn_ref, wg_ref, wu_ref, wd_ref,
                  gple_ref, wpg_ref, wpp_ref, gfin_ref,
                  o_ref, state_ref, zbuf_ref, y_ref):
    ts = SEQ_TILE
    s = pl.program_id(1)

    @pl.when(s == 0)
    def _():
        state_ref[...] = jnp.zeros_like(state_ref)
        zbuf_ref[0:HIST_ROWS, :] = jnp.zeros((HIST_ROWS, CONV_WIDTH), F32)

    @pl.when(s > 0)
    def _():
        zbuf_ref[0:HIST_ROWS, :] = zbuf_ref[ts:ts + HIST_ROWS, :]

    x = x_ref[0]
    u = _rmsnorm(x, gmix_ref[...]).astype(BF16)

    pc = jnp.dot(u, win_ref[:, 0:CONV_COLS], preferred_element_type=F32)
    cb = pc[:, 0:CONV_WIDTH]
    z = pc[:, CONV_WIDTH:2 * CONV_WIDTH] * pc[:, 2 * CONV_WIDTH:3 * CONV_WIDTH]
    zbuf_ref[HIST_ROWS:HIST_ROWS + ts, :] = z
    z1 = zbuf_ref[HIST_ROWS - 1:HIST_ROWS - 1 + ts, :]
    z2 = zbuf_ref[HIST_ROWS - 2:HIST_ROWS - 2 + ts, :]
    conv = convw_ref[0:1, :] * z2 + convw_ref[1:2, :] * z1 + convw_ref[2:3, :] * z
    y_ref[:, 0:CONV_WIDTH] = (cb * conv).astype(BF16)

    pr = jnp.dot(u, win_ref[:, CONV_COLS:IN_COLS], preferred_element_type=F32)
    half = RET_HEAD_DIM // 2
    for h in range(RET_HEADS):
        lo = h * RET_HEAD_DIM
        hi = lo + RET_HEAD_DIM
        st = state_ref[h]
        for c in range(ts // RET_CHUNK):
            r0 = c * RET_CHUNK
            r1 = r0 + RET_CHUNK
            q = pr[r0:r1, lo:hi]
            k = pr[r0:r1, RET_WIDTH + lo:RET_WIDTH + hi]
            v = pr[r0:r1, 2 * RET_WIDTH + lo:2 * RET_WIDTH + hi]
            g = pr[r0:r1, 3 * RET_WIDTH + lo:3 * RET_WIDTH + hi]
            qr = q * cq_ref[r0:r1, :] + pltpu.roll(q, half, 1) * sq_ref[r0:r1, :]
            kr = k * ck_ref[r0:r1, :] + pltpu.roll(k, half, 1) * sk_ref[r0:r1, :]
            qb = qr.astype(BF16)
            kb = kr.astype(BF16)
            vb = v.astype(BF16)
            scores = lax.dot_general(qb, kb, (((1,), (1,)), ((), ())),
                                     preferred_element_type=F32)
            intra = jnp.dot((scores * mask_ref[h]).astype(BF16), vb,
                            preferred_element_type=F32)
            cross = jnp.dot((qr * qd_ref[h]).astype(BF16), st.astype(BF16),
                            preferred_element_type=F32)
            o = intra + cross
            kv = lax.dot_general((kr * kd_ref[h]).astype(BF16), vb,
                                 (((0,), (0,)), ((), ())), preferred_element_type=F32)
            st = st * cd_ref[h] + kv
            mu = jnp.mean(o, axis=-1, keepdims=True)
            d = o - mu
            var = jnp.mean(d * d, axis=-1, keepdims=True)
            yn = d * lax.rsqrt(var + EPS) * gn_ref[:, lo:hi]
            y_ref[r0:r1, CONV_WIDTH + lo:CONV_WIDTH + hi] = (_silu(g) * yn).astype(BF16)
        state_ref[h] = st

    h1 = x + jnp.dot(y_ref[...], wout_ref[...], preferred_element_type=F32)

    u2 = _rmsnorm(h1, gffn_ref[...]).astype(BF16)
    gate = jnp.dot(u2, wg_ref[...], preferred_element_type=F32)
    up = jnp.dot(u2, wu_ref[...], preferred_element_type=F32)
    hid = (_silu(gate) * up).astype(BF16)
    h2 = h1 + jnp.dot(hid, wd_ref[...], preferred_element_type=F32)

    u3 = _rmsnorm(h2, gple_ref[...]).astype(BF16)
    pgate = jax.nn.sigmoid(jnp.dot(u3, wpg_ref[...], preferred_element_type=F32))
    pproj = jnp.dot(p_ref[0].astype(BF16), wpp_ref[...], preferred_element_type=F32)
    h3 = h2 + pgate * pproj

    o_ref[0] = _rmsnorm(h3, gfin_ref[...])


def _rope_tables(seq, k_scale):
    half = RET_HEAD_DIM // 2
    pos = jnp.arange(seq, dtype=F32)
    inv_freq = ROPE_BASE ** (-jnp.arange(half, dtype=F32) / half)
    ang = pos[:, None] * inv_freq[None, :]
    cos = jnp.cos(ang)
    sin = jnp.sin(ang)
    cos_full = jnp.concatenate([cos, cos], axis=-1)
    sin_full = jnp.concatenate([-sin, sin], axis=-1)
    return cos_full, sin_full, cos_full * k_scale, sin_full * k_scale


def _decay_tables(chunk):
    hh = jnp.arange(RET_HEADS, dtype=F32)
    log_gamma = jnp.log(1.0 - jnp.power(2.0, -5.0 - hh))
    idx = jnp.arange(chunk, dtype=F32)
    diff = idx[:, None] - idx[None, :]
    mask = jnp.where(diff[None] >= 0,
                     jnp.exp(log_gamma[:, None, None] * jnp.maximum(diff, 0.0)[None]),
                     0.0)
    q_decay = jnp.exp(log_gamma[:, None] * (idx + 1.0)[None])
    k_decay = jnp.exp(log_gamma[:, None] * (chunk - 1 - idx)[None])
    chunk_decay = jnp.exp(log_gamma * chunk)
    lanes = (RET_HEADS, chunk, RET_HEAD_DIM)
    qd = jnp.broadcast_to(q_decay[:, :, None], lanes)
    kd = jnp.broadcast_to(k_decay[:, :, None], lanes)
    cd = jnp.broadcast_to(chunk_decay[:, None, None], (RET_HEADS, 1, RET_HEAD_DIM))
    return mask, qd, kd, cd


def kernel(x, p, g_mix, w_in, conv_w, ret_gn, w_out, g_ffn, w_gate, w_up, w_down,
           g_ple, w_ple_gate, w_ple_proj, g_final):
    b, s, d = x.shape
    assert d == D_MODEL and s % SEQ_TILE == 0 and SEQ_TILE % RET_CHUNK == 0
    assert p.shape[0] == 1, "single-layer block"
    ts = SEQ_TILE

    cq, sq, ck, sk = _rope_tables(s, RET_HEAD_DIM ** -0.5)
    mask, qd, kd, cd = _decay_tables(RET_CHUNK)

    row = lambda a: a.reshape(1, -1).astype(F32)
    whole = pl.BlockSpec(memory_space=pltpu.VMEM)
    rope_spec = pl.BlockSpec((ts, RET_HEAD_DIM), lambda bi, si: (si, 0))

    operands = [
        (x, pl.BlockSpec((1, ts, D_MODEL), lambda bi, si: (bi, si, 0))),
        (p[0], pl.BlockSpec((1, ts, PLE_DIM), lambda bi, si: (bi, si, 0))),
        (cq, rope_spec), (sq, rope_spec), (ck, rope_spec), (sk, rope_spec),
        (mask, whole), (qd, whole), (kd, whole), (cd, whole),
        (row(g_mix[0]), whole), (w_in[0].astype(BF16), whole),
        (conv_w[0].astype(F32), whole), (row(ret_gn[0]), whole),
        (w_out[0].astype(BF16), whole),
        (row(g_ffn[0]), whole), (w_gate[0].astype(BF16), whole),
        (w_up[0].astype(BF16), whole), (w_down[0].astype(BF16), whole),
        (row(g_ple[0]), whole), (w_ple_gate[0].astype(BF16), whole),
        (w_ple_proj[0].astype(BF16), whole), (row(g_final), whole),
    ]
    args = [a for a, _ in operands]
    in_specs = [sp for _, sp in operands]

    return pl.pallas_call(
        _block_kernel,
        grid=(b, s // ts),
        in_specs=in_specs,
        out_specs=pl.BlockSpec((1, ts, D_MODEL), lambda bi, si: (bi, si, 0)),
        out_shape=jax.ShapeDtypeStruct((b, s, D_MODEL), x.dtype),
        scratch_shapes=[
            pltpu.VMEM((RET_HEADS, RET_HEAD_DIM, RET_HEAD_DIM), F32),
            pltpu.VMEM((HIST_ROWS + ts, CONV_WIDTH), F32),
            pltpu.VMEM((ts, MIX_WIDTH), BF16),
        ],
        compiler_params=pltpu.CompilerParams(
            dimension_semantics=("arbitrary", "arbitrary"),
            vmem_limit_bytes=VMEM_LIMIT_BYTES),
        name="hybrid_block",
    )(*args)
```

```python
import jax
import jax.numpy as jnp
from jax import lax
from jax.experimental import pallas as pl
from jax.experimental.pallas import tpu as pltpu

D_MODEL = 1024
PLE_DIM = 256
CONV_WIDTH = 512
CONV_K = 3
RET_HEADS = 4
RET_HEAD_DIM = 128
RET_WIDTH = RET_HEADS * RET_HEAD_DIM
MIX_WIDTH = CONV_WIDTH + RET_WIDTH
CONV_COLS = 3 * CONV_WIDTH
IN_COLS = CONV_COLS + 4 * RET_WIDTH
D_FF = 2816
ROPE_BASE = 10000.0
EPS = 1e-6

SEQ_TILE = 512
SUB_ROWS = 256
RET_CHUNK = 256
HIST_ROWS = 8
VMEM_LIMIT_BYTES = 60 * 1024 * 1024

F32 = jnp.float32
BF16 = jnp.bfloat16


def _rmsnorm(x, g):
    return x * lax.rsqrt(jnp.mean(x * x, axis=-1, keepdims=True) + EPS) * g


def _silu(x):
    return x * jax.nn.sigmoid(x)


def _block_kernel(x_ref, p_ref, cq_ref, sq_ref, ck_ref, sk_ref,
                  mask_ref, qd_ref, kd_ref, cd_ref,
                  gmix_ref, win_ref, convw_ref, gn_ref, wout_ref,
                  gffn_ref, wg_ref, wu_ref, wd_ref,
                  gple_ref, wpg_ref, wpp_ref, gfin_ref,
                  o_ref, state_ref, zbuf_ref, y_ref):
    ts = SEQ_TILE
    s = pl.program_id(1)

    @pl.when(s == 0)
    def _():
        state_ref[...] = jnp.zeros_like(state_ref)
        zbuf_ref[0:HIST_ROWS, :] = jnp.zeros((HIST_ROWS, CONV_WIDTH), F32)

    @pl.when(s > 0)
    def _():
        zbuf_ref[0:HIST_ROWS, :] = zbuf_ref[ts:ts + HIST_ROWS, :]

    subs = [(i * SUB_ROWS, (i + 1) * SUB_ROWS) for i in range(ts // SUB_ROWS)]
    half = RET_HEAD_DIM // 2

    def in_proj(r0, r1):
        x = x_ref[0, r0:r1, :]
        u = _rmsnorm(x, gmix_ref[...]).astype(BF16)
        pc = jnp.dot(u, win_ref[:, 0:CONV_COLS], preferred_element_type=F32)
        pr = jnp.dot(u, win_ref[:, CONV_COLS:IN_COLS], preferred_element_type=F32)
        return pc, pr

    def conv_mixer(r0, r1, pc):
        cb = pc[:, 0:CONV_WIDTH]
        z = pc[:, CONV_WIDTH:2 * CONV_WIDTH] * pc[:, 2 * CONV_WIDTH:3 * CONV_WIDTH]
        zbuf_ref[HIST_ROWS + r0:HIST_ROWS + r1, :] = z
        z1 = zbuf_ref[HIST_ROWS - 1 + r0:HIST_ROWS - 1 + r1, :]
        z2 = zbuf_ref[HIST_ROWS - 2 + r0:HIST_ROWS - 2 + r1, :]
        conv = convw_ref[0:1, :] * z2 + convw_ref[1:2, :] * z1 + convw_ref[2:3, :] * z
        y_ref[r0:r1, 0:CONV_WIDTH] = (cb * conv).astype(BF16)

    def retention(r0, r1, pr, states):
        new_states = []
        for h in range(RET_HEADS):
            lo = h * RET_HEAD_DIM
            hi = lo + RET_HEAD_DIM
            st = states[h]
            for c0 in range(0, r1 - r0, RET_CHUNK):
                c1 = c0 + RET_CHUNK
                q = pr[c0:c1, lo:hi]
                k = pr[c0:c1, RET_WIDTH + lo:RET_WIDTH + hi]
                v = pr[c0:c1, 2 * RET_WIDTH + lo:2 * RET_WIDTH + hi]
                g = pr[c0:c1, 3 * RET_WIDTH + lo:3 * RET_WIDTH + hi]
                t0, t1 = r0 + c0, r0 + c1
                qr = q * cq_ref[t0:t1, :] + pltpu.roll(q, half, 1) * sq_ref[t0:t1, :]
                kr = k * ck_ref[t0:t1, :] + pltpu.roll(k, half, 1) * sk_ref[t0:t1, :]
                qb = qr.astype(BF16)
                kb = kr.astype(BF16)
                vb = v.astype(BF16)
                scores = lax.dot_general(qb, kb, (((1,), (1,)), ((), ())),
                                         preferred_element_type=F32)
                intra = jnp.dot((scores * mask_ref[h]).astype(BF16), vb,
                                preferred_element_type=F32)
                cross = jnp.dot((qr * qd_ref[h]).astype(BF16), st.astype(BF16),
                                preferred_element_type=F32)
                o = intra + cross
                kv = lax.dot_general((kr * kd_ref[h]).astype(BF16), vb,
                                     (((0,), (0,)), ((), ())), preferred_element_type=F32)
                st = st * cd_ref[h] + kv
                mu = jnp.mean(o, axis=-1, keepdims=True)
                d = o - mu
                var = jnp.mean(d * d, axis=-1, keepdims=True)
                yn = d * lax.rsqrt(var + EPS) * gn_ref[:, lo:hi]
                y_ref[t0:t1, CONV_WIDTH + lo:CONV_WIDTH + hi] = (_silu(g) * yn).astype(BF16)
            new_states.append(st)
        return new_states

    def out_proj(r0, r1):
        return x_ref[0, r0:r1, :] + jnp.dot(y_ref[r0:r1, :], wout_ref[...],
                                            preferred_element_type=F32)

    def ffn_hidden(h1):
        u2 = _rmsnorm(h1, gffn_ref[...]).astype(BF16)
        gate = jnp.dot(u2, wg_ref[...], preferred_element_type=F32)
        up = jnp.dot(u2, wu_ref[...], preferred_element_type=F32)
        return (_silu(gate) * up).astype(BF16)

    def ffn_down(h1, hid):
        return h1 + jnp.dot(hid, wd_ref[...], preferred_element_type=F32)

    def ple(r0, r1, h2):
        u3 = _rmsnorm(h2, gple_ref[...]).astype(BF16)
        pgate = jax.nn.sigmoid(jnp.dot(u3, wpg_ref[...], preferred_element_type=F32))
        pproj = jnp.dot(p_ref[0, r0:r1, :].astype(BF16), wpp_ref[...],
                        preferred_element_type=F32)
        return h2 + pgate * pproj

    proj = [in_proj(r0, r1) for r0, r1 in subs]
    states = [state_ref[h] for h in range(RET_HEADS)]
    for (r0, r1), (pc, pr) in zip(subs, proj):
        conv_mixer(r0, r1, pc)
        states = retention(r0, r1, pr, states)
    for h in range(RET_HEADS):
        state_ref[h] = states[h]
    h1 = [out_proj(r0, r1) for r0, r1 in subs]
    hid = [ffn_hidden(a) for a in h1]
    h2 = [ffn_down(a, b) for a, b in zip(h1, hid)]
    h3 = [ple(r0, r1, a) for (r0, r1), a in zip(subs, h2)]
    for (r0, r1), a in zip(subs, h3):
        o_ref[0, r0:r1, :] = _rmsnorm(a, gfin_ref[...])


def _rope_tables(seq, k_scale):
    half = RET_HEAD_DIM // 2
    pos = jnp.arange(seq, dtype=F32)
    inv_freq = ROPE_BASE ** (-jnp.arange(half, dtype=F32) / half)
    ang = pos[:, None] * inv_freq[None, :]
    cos = jnp.cos(ang)
    sin = jnp.sin(ang)
    cos_full = jnp.concatenate([cos, cos], axis=-1)
    sin_full = jnp.concatenate([-sin, sin], axis=-1)
    return cos_full, sin_full, cos_full * k_scale, sin_full * k_scale


def _decay_tables(chunk):
    hh = jnp.arange(RET_HEADS, dtype=F32)
    log_gamma = jnp.log(1.0 - jnp.power(2.0, -5.0 - hh))
    idx = jnp.arange(chunk, dtype=F32)
    diff = idx[:, None] - idx[None, :]
    mask = jnp.where(diff[None] >= 0,
                     jnp.exp(log_gamma[:, None, None] * jnp.maximum(diff, 0.0)[None]),
                     0.0)
    q_decay = jnp.exp(log_gamma[:, None] * (idx + 1.0)[None])
    k_decay = jnp.exp(log_gamma[:, None] * (chunk - 1 - idx)[None])
    chunk_decay = jnp.exp(log_gamma * chunk)
    lanes = (RET_HEADS, chunk, RET_HEAD_DIM)
    qd = jnp.broadcast_to(q_decay[:, :, None], lanes)
    kd = jnp.broadcast_to(k_decay[:, :, None], lanes)
    cd = jnp.broadcast_to(chunk_decay[:, None, None], (RET_HEADS, 1, RET_HEAD_DIM))
    return mask, qd, kd, cd


def kernel(x, p, g_mix, w_in, conv_w, ret_gn, w_out, g_ffn, w_gate, w_up, w_down,
           g_ple, w_ple_gate, w_ple_proj, g_final):
    b, s, d = x.shape
    assert d == D_MODEL and s % SEQ_TILE == 0
    assert SEQ_TILE % SUB_ROWS == 0 and SUB_ROWS % RET_CHUNK == 0
    assert p.shape[0] == 1, "single-layer block"
    ts = SEQ_TILE

    cq, sq, ck, sk = _rope_tables(s, RET_HEAD_DIM ** -0.5)
    mask, qd, kd, cd = _decay_tables(RET_CHUNK)

    row = lambda a: a.reshape(1, -1).astype(F32)
    whole = pl.BlockSpec(memory_space=pltpu.VMEM)
    rope_spec = pl.BlockSpec((ts, RET_HEAD_DIM), lambda bi, si: (si, 0))

    operands = [
        (x, pl.BlockSpec((1, ts, D_MODEL), lambda bi, si: (bi, si, 0))),
        (p[0], pl.BlockSpec((1, ts, PLE_DIM), lambda bi, si: (bi, si, 0))),
        (cq, rope_spec), (sq, rope_spec), (ck, rope_spec), (sk, rope_spec),
        (mask, whole), (qd, whole), (kd, whole), (cd, whole),
        (row(g_mix[0]), whole), (w_in[0].astype(BF16), whole),
        (conv_w[0].astype(F32), whole), (row(ret_gn[0]), whole),
        (w_out[0].astype(BF16), whole),
        (row(g_ffn[0]), whole), (w_gate[0].astype(BF16), whole),
        (w_up[0].astype(BF16), whole), (w_down[0].astype(BF16), whole),
        (row(g_ple[0]), whole), (w_ple_gate[0].astype(BF16), whole),
        (w_ple_proj[0].astype(BF16), whole), (row(g_final), whole),
    ]
    args = [a for a, _ in operands]
    in_specs = [sp for _, sp in operands]

    return pl.pallas_call(
        _block_kernel,
        grid=(b, s // ts),
        in_specs=in_specs,
        out_specs=pl.BlockSpec((1, ts, D_MODEL), lambda bi, si: (bi, si, 0)),
        out_shape=jax.ShapeDtypeStruct((b, s, D_MODEL), x.dtype),
        scratch_shapes=[
            pltpu.VMEM((RET_HEADS, RET_HEAD_DIM, RET_HEAD_DIM), F32),
            pltpu.VMEM((HIST_ROWS + ts, CONV_WIDTH), F32),
            pltpu.VMEM((ts, MIX_WIDTH), BF16),
        ],
        compiler_params=pltpu.CompilerParams(
            dimension_semantics=("arbitrary", "arbitrary"),
            vmem_limit_bytes=VMEM_LIMIT_BYTES),
        name="hybrid_block",
    )(*args)
```

```python
import functools

import jax
import jax.numpy as jnp
from jax import lax
from jax.experimental import pallas as pl
from jax.experimental.pallas import tpu as pltpu

D_MODEL = 1024
PLE_DIM = 256
CONV_WIDTH = 512
CONV_K = 3
RET_HEADS = 4
RET_HEAD_DIM = 128
RET_WIDTH = RET_HEADS * RET_HEAD_DIM
MIX_WIDTH = CONV_WIDTH + RET_WIDTH
CONV_COLS = 3 * CONV_WIDTH
IN_COLS = CONV_COLS + 4 * RET_WIDTH
D_FF = 2816
ROPE_BASE = 10000.0
EPS = 1e-6

SEQ_TILE = 512
SUB_ROWS = 256
RET_CHUNK = 256
HIST_ROWS = 8
LANES = 128
VMEM_LIMIT_BYTES = 60 * 1024 * 1024

F32 = jnp.float32
BF16 = jnp.bfloat16


def _rmsnorm(x, g):
    return x * lax.rsqrt(jnp.mean(x * x, axis=-1, keepdims=True) + EPS) * g


def _silu(x):
    return x * jax.nn.sigmoid(x)


def _block_kernel(n_seq_tiles, n_tiles,
                  x_ref, p_ref, cq_ref, sq_ref, ck_ref, sk_ref,
                  mask_ref, qd_ref, kd_ref, cd_ref,
                  gmix_ref, win_ref, convw_ref, gn_ref, wout_ref,
                  gffn_ref, wg_ref, wu_ref, wd_ref,
                  gple_ref, wpg_ref, wpp_ref, gfin_ref,
                  o_ref, state_ref, zbuf_ref, y_ref, h1_ref, u2_ref):
    ts = SEQ_TILE
    t = pl.program_id(0)
    seq_start = lax.rem(t, n_seq_tiles) == 0

    @pl.when(seq_start)
    def _():
        state_ref[...] = jnp.zeros_like(state_ref)
        zbuf_ref[:, 0:HIST_ROWS, :] = jnp.zeros((CONV_WIDTH // LANES, HIST_ROWS, LANES), F32)

    @pl.when(jnp.logical_not(seq_start))
    def _():
        zbuf_ref[:, 0:HIST_ROWS, :] = zbuf_ref[:, ts:ts + HIST_ROWS, :]

    subs = [(i * SUB_ROWS, (i + 1) * SUB_ROWS) for i in range(ts // SUB_ROWS)]
    half = RET_HEAD_DIM // 2

    def in_proj(r0, r1):
        x = x_ref[0, r0:r1, :]
        u = _rmsnorm(x, gmix_ref[...]).astype(BF16)
        pc = jnp.dot(u, win_ref[:, 0:CONV_COLS], preferred_element_type=F32)
        pr = jnp.dot(u, win_ref[:, CONV_COLS:IN_COLS], preferred_element_type=F32)
        return pc, pr

    def conv_mixer(r0, r1, pc):
        for c in range(CONV_WIDTH // LANES):
            lo = c * LANES
            hi = lo + LANES
            z = pc[:, CONV_WIDTH + lo:CONV_WIDTH + hi] * pc[:, 2 * CONV_WIDTH + lo:2 * CONV_WIDTH + hi]
            zbuf_ref[c, HIST_ROWS + r0:HIST_ROWS + r1, :] = z
            z1 = zbuf_ref[c, HIST_ROWS - 1 + r0:HIST_ROWS - 1 + r1, :]
            z2 = zbuf_ref[c, HIST_ROWS - 2 + r0:HIST_ROWS - 2 + r1, :]
            conv = (convw_ref[0:1, lo:hi] * z2 + convw_ref[1:2, lo:hi] * z1
                    + convw_ref[2:3, lo:hi] * z)
            y_ref[r0:r1, lo:hi] = (pc[:, lo:hi] * conv).astype(BF16)

    def retention(r0, r1, pr, states):
        new_states = []
        for h in range(RET_HEADS):
            lo = h * RET_HEAD_DIM
            hi = lo + RET_HEAD_DIM
            st = states[h]
            for c0 in range(0, r1 - r0, RET_CHUNK):
                c1 = c0 + RET_CHUNK
                q = pr[c0:c1, lo:hi]
                k = pr[c0:c1, RET_WIDTH + lo:RET_WIDTH + hi]
                v = pr[c0:c1, 2 * RET_WIDTH + lo:2 * RET_WIDTH + hi]
                g = pr[c0:c1, 3 * RET_WIDTH + lo:3 * RET_WIDTH + hi]
                t0, t1 = r0 + c0, r0 + c1
                qr = q * cq_ref[t0:t1, :] + pltpu.roll(q, half, 1) * sq_ref[t0:t1, :]
                kr = k * ck_ref[t0:t1, :] + pltpu.roll(k, half, 1) * sk_ref[t0:t1, :]
                qb = qr.astype(BF16)
                kb = kr.astype(BF16)
                vb = v.astype(BF16)
                scores = lax.dot_general(qb, kb, (((1,), (1,)), ((), ())),
                                         preferred_element_type=F32)
                intra = jnp.dot((scores * mask_ref[h]).astype(BF16), vb,
                                preferred_element_type=F32)
                cross = jnp.dot((qr * qd_ref[h]).astype(BF16), st.astype(BF16),
                                preferred_element_type=F32)
                o = intra + cross
                kv = lax.dot_general((kr * kd_ref[h]).astype(BF16), vb,
                                     (((0,), (0,)), ((), ())), preferred_element_type=F32)
                st = st * cd_ref[h] + kv
                mu = jnp.mean(o, axis=-1, keepdims=True)
                d = o - mu
                var = jnp.mean(d * d, axis=-1, keepdims=True)
                yn = d * lax.rsqrt(var + EPS) * gn_ref[:, lo:hi]
                y_ref[t0:t1, CONV_WIDTH + lo:CONV_WIDTH + hi] = (_silu(g) * yn).astype(BF16)
            new_states.append(st)
        return new_states

    def out_proj(r0, r1):
        return x_ref[0, r0:r1, :] + jnp.dot(y_ref[r0:r1, :], wout_ref[...],
                                            preferred_element_type=F32)

    def ffn_input(i, r0, r1):
        if i == 0:
            return u2_ref[...]
        return _rmsnorm(h1_ref[r0:r1, :], gffn_ref[...]).astype(BF16)

    def ffn_hidden(u2):
        gate = jnp.dot(u2, wg_ref[...], preferred_element_type=F32)
        up = jnp.dot(u2, wu_ref[...], preferred_element_type=F32)
        return (_silu(gate) * up).astype(BF16)

    def ffn_down(r0, r1, hid):
        return h1_ref[r0:r1, :] + jnp.dot(hid, wd_ref[...], preferred_element_type=F32)

    def ple(r0, r1, h2):
        u3 = _rmsnorm(h2, gple_ref[...]).astype(BF16)
        pgate = jax.nn.sigmoid(jnp.dot(u3, wpg_ref[...], preferred_element_type=F32))
        pproj = jnp.dot(p_ref[0, r0:r1, :].astype(BF16), wpp_ref[...],
                        preferred_element_type=F32)
        return h2 + pgate * pproj

    def program(mix, chan):
        n = len(subs)
        hid, h2, proj = [None] * n, [None] * n, [None] * n
        if chan:
            u2 = [ffn_input(i, r0, r1) for i, (r0, r1) in enumerate(subs)]
        for i, (r0, r1) in enumerate(subs):
            if chan:
                hid[i] = ffn_hidden(u2[i])
            if mix:
                proj[i] = in_proj(r0, r1)
        if mix:
            states = [state_ref[h] for h in range(RET_HEADS)]
        for i, (r0, r1) in enumerate(subs):
            if chan:
                h2[i] = ffn_down(r0, r1, hid[i])
            if mix:
                conv_mixer(r0, r1, proj[i][0])
                states = retention(r0, r1, proj[i][1], states)
        if mix:
            for h in range(RET_HEADS):
                state_ref[h] = states[h]
        for i, (r0, r1) in enumerate(subs):
            if chan:
                o_ref[0, r0:r1, :] = _rmsnorm(ple(r0, r1, h2[i]), gfin_ref[...])
            if mix:
                h1 = out_proj(r0, r1)
                h1_ref[r0:r1, :] = h1
                if i == 0:
                    u2_ref[...] = _rmsnorm(h1, gffn_ref[...]).astype(BF16)

    @pl.when(t == 0)
    def _():
        program(mix=True, chan=False)

    @pl.when(jnp.logical_and(t > 0, t < n_tiles))
    def _():
        program(mix=True, chan=True)

    @pl.when(t == n_tiles)
    def _():
        program(mix=False, chan=True)


def _rope_tables(seq, k_scale):
    half = RET_HEAD_DIM // 2
    pos = jnp.arange(seq, dtype=F32)
    inv_freq = ROPE_BASE ** (-jnp.arange(half, dtype=F32) / half)
    ang = pos[:, None] * inv_freq[None, :]
    cos = jnp.cos(ang)
    sin = jnp.sin(ang)
    cos_full = jnp.concatenate([cos, cos], axis=-1)
    sin_full = jnp.concatenate([-sin, sin], axis=-1)
    return cos_full, sin_full, cos_full * k_scale, sin_full * k_scale


def _decay_tables(chunk):
    hh = jnp.arange(RET_HEADS, dtype=F32)
    log_gamma = jnp.log(1.0 - jnp.power(2.0, -5.0 - hh))
    idx = jnp.arange(chunk, dtype=F32)
    diff = idx[:, None] - idx[None, :]
    mask = jnp.where(diff[None] >= 0,
                     jnp.exp(log_gamma[:, None, None] * jnp.maximum(diff, 0.0)[None]),
                     0.0)
    q_decay = jnp.exp(log_gamma[:, None] * (idx + 1.0)[None])
    k_decay = jnp.exp(log_gamma[:, None] * (chunk - 1 - idx)[None])
    chunk_decay = jnp.exp(log_gamma * chunk)
    lanes = (RET_HEADS, chunk, RET_HEAD_DIM)
    qd = jnp.broadcast_to(q_decay[:, :, None], lanes)
    kd = jnp.broadcast_to(k_decay[:, :, None], lanes)
    cd = jnp.broadcast_to(chunk_decay[:, None, None], (RET_HEADS, 1, RET_HEAD_DIM))
    return mask, qd, kd, cd


def kernel(x, p, g_mix, w_in, conv_w, ret_gn, w_out, g_ffn, w_gate, w_up, w_down,
           g_ple, w_ple_gate, w_ple_proj, g_final):
    b, s, d = x.shape
    assert d == D_MODEL and s % SEQ_TILE == 0
    assert SEQ_TILE % SUB_ROWS == 0 and SUB_ROWS % RET_CHUNK == 0
    assert p.shape[0] == 1, "single-layer block"
    ts = SEQ_TILE
    ns = s // ts
    n_tiles = b * ns

    cq, sq, ck, sk = _rope_tables(s, RET_HEAD_DIM ** -0.5)
    mask, qd, kd, cd = _decay_tables(RET_CHUNK)

    def mix_tile(t):
        tm = jnp.minimum(t, n_tiles - 1)
        return tm // ns, tm % ns

    def chan_tile(t):
        tc = jnp.maximum(t - 1, 0)
        return tc // ns, tc % ns

    row = lambda a: a.reshape(1, -1).astype(F32)
    whole = pl.BlockSpec(memory_space=pltpu.VMEM)
    rope_spec = pl.BlockSpec((ts, RET_HEAD_DIM), lambda t: (mix_tile(t)[1], 0))

    operands = [
        (x, pl.BlockSpec((1, ts, D_MODEL), lambda t: (*mix_tile(t), 0))),
        (p[0], pl.BlockSpec((1, ts, PLE_DIM), lambda t: (*chan_tile(t), 0))),
        (cq, rope_spec), (sq, rope_spec), (ck, rope_spec), (sk, rope_spec),
        (mask, whole), (qd, whole), (kd, whole), (cd, whole),
        (row(g_mix[0]), whole), (w_in[0].astype(BF16), whole),
        (conv_w[0].astype(F32), whole), (row(ret_gn[0]), whole),
        (w_out[0].astype(BF16), whole),
        (row(g_ffn[0]), whole), (w_gate[0].astype(BF16), whole),
        (w_up[0].astype(BF16), whole), (w_down[0].astype(BF16), whole),
        (row(g_ple[0]), whole), (w_ple_gate[0].astype(BF16), whole),
        (w_ple_proj[0].astype(BF16), whole), (row(g_final), whole),
    ]
    args = [a for a, _ in operands]
    in_specs = [sp for _, sp in operands]

    return pl.pallas_call(
        functools.partial(_block_kernel, ns, n_tiles),
        grid=(n_tiles + 1,),
        in_specs=in_specs,
        out_specs=pl.BlockSpec((1, ts, D_MODEL), lambda t: (*chan_tile(t), 0)),
        out_shape=jax.ShapeDtypeStruct((b, s, D_MODEL), x.dtype),
        scratch_shapes=[
            pltpu.VMEM((RET_HEADS, RET_HEAD_DIM, RET_HEAD_DIM), F32),
            pltpu.VMEM((CONV_WIDTH // LANES, HIST_ROWS + ts, LANES), F32),
            pltpu.VMEM((ts, MIX_WIDTH), BF16),
            pltpu.VMEM((ts, D_MODEL), F32),
            pltpu.VMEM((SUB_ROWS, D_MODEL), BF16),
        ],
        compiler_params=pltpu.CompilerParams(
            dimension_semantics=("arbitrary",),
            vmem_limit_bytes=VMEM_LIMIT_BYTES),
        name="hybrid_block",
    )(*args)
```

```python
import jax
import jax.numpy as jnp
from jax import lax
from jax.experimental import pallas as pl
from jax.experimental.pallas import tpu as pltpu

D_MODEL = 1024
PLE_DIM = 256
CONV_WIDTH = 512
CONV_K = 3
RET_HEADS = 4
RET_HEAD_DIM = 128
RET_WIDTH = RET_HEADS * RET_HEAD_DIM
MIX_WIDTH = CONV_WIDTH + RET_WIDTH
CONV_COLS = 3 * CONV_WIDTH
IN_COLS = CONV_COLS + 4 * RET_WIDTH
D_FF = 2816
ROPE_BASE = 10000.0
EPS = 1e-6

SEQ_TILE = 512
SUB_ROWS = 256
RET_CHUNK = 256
HIST_ROWS = 8
LANES = 128
VMEM_LIMIT_BYTES = 60 * 1024 * 1024

F32 = jnp.float32
BF16 = jnp.bfloat16


def _rmsnorm(x, g):
    return x * lax.rsqrt(jnp.mean(x * x, axis=-1, keepdims=True) + EPS) * g


def _silu(x):
    return x * jax.nn.sigmoid(x)


def _block_kernel(x_ref, p_ref, cq_ref, sq_ref, ck_ref, sk_ref,
                  mask_ref, qd_ref, kd_ref, cd_ref,
                  gmix_ref, win_ref, convw_ref, gn_ref, wout_ref,
                  gffn_ref, wg_ref, wu_ref, wd_ref,
                  gple_ref, wpg_ref, wpp_ref, gfin_ref,
                  o_ref, state_ref, zbuf_ref, y_ref):
    ts = SEQ_TILE
    s = pl.program_id(1)

    @pl.when(s == 0)
    def _():
        state_ref[...] = jnp.zeros_like(state_ref)
        zbuf_ref[:, 0:HIST_ROWS, :] = jnp.zeros((CONV_WIDTH // LANES, HIST_ROWS, LANES), F32)

    @pl.when(s > 0)
    def _():
        zbuf_ref[:, 0:HIST_ROWS, :] = zbuf_ref[:, ts:ts + HIST_ROWS, :]

    subs = [(i * SUB_ROWS, (i + 1) * SUB_ROWS) for i in range(ts // SUB_ROWS)]
    half = RET_HEAD_DIM // 2

    def in_proj(r0, r1):
        x = x_ref[0, r0:r1, :]
        u = _rmsnorm(x, gmix_ref[...]).astype(BF16)
        pc = jnp.dot(u, win_ref[:, 0:CONV_COLS], preferred_element_type=F32)
        pr = jnp.dot(u, win_ref[:, CONV_COLS:IN_COLS], preferred_element_type=F32)
        return pc, pr

    def conv_mixer(r0, r1, pc):
        for c in range(CONV_WIDTH // LANES):
            lo = c * LANES
            hi = lo + LANES
            z = pc[:, CONV_WIDTH + lo:CONV_WIDTH + hi] * pc[:, 2 * CONV_WIDTH + lo:2 * CONV_WIDTH + hi]
            zbuf_ref[c, HIST_ROWS + r0:HIST_ROWS + r1, :] = z
            z1 = zbuf_ref[c, HIST_ROWS - 1 + r0:HIST_ROWS - 1 + r1, :]
            z2 = zbuf_ref[c, HIST_ROWS - 2 + r0:HIST_ROWS - 2 + r1, :]
            conv = (convw_ref[0:1, lo:hi] * z2 + convw_ref[1:2, lo:hi] * z1
                    + convw_ref[2:3, lo:hi] * z)
            y_ref[r0:r1, lo:hi] = (pc[:, lo:hi] * conv).astype(BF16)

    def retention(r0, r1, pr, states):
        new_states = []
        for h in range(RET_HEADS):
            lo = h * RET_HEAD_DIM
            hi = lo + RET_HEAD_DIM
            st = states[h]
            for c0 in range(0, r1 - r0, RET_CHUNK):
                c1 = c0 + RET_CHUNK
                q = pr[c0:c1, lo:hi]
                k = pr[c0:c1, RET_WIDTH + lo:RET_WIDTH + hi]
                v = pr[c0:c1, 2 * RET_WIDTH + lo:2 * RET_WIDTH + hi]
                g = pr[c0:c1, 3 * RET_WIDTH + lo:3 * RET_WIDTH + hi]
                t0, t1 = r0 + c0, r0 + c1
                qr = q * cq_ref[t0:t1, :] + pltpu.roll(q, half, 1) * sq_ref[t0:t1, :]
                kr = k * ck_ref[t0:t1, :] + pltpu.roll(k, half, 1) * sk_ref[t0:t1, :]
                qb = qr.astype(BF16)
                kb = kr.astype(BF16)
                vb = v.astype(BF16)
                scores = lax.dot_general(qb, kb, (((1,), (1,)), ((), ())),
                                         preferred_element_type=F32)
                intra = jnp.dot((scores * mask_ref[h]).astype(BF16), vb,
                                preferred_element_type=F32)
                cross = jnp.dot((qr * qd_ref[h]).astype(BF16), st.astype(BF16),
                                preferred_element_type=F32)
                o = intra + cross
                kv = lax.dot_general((kr * kd_ref[h]).astype(BF16), vb,
                                     (((0,), (0,)), ((), ())), preferred_element_type=F32)
                st = st * cd_ref[h] + kv
                mu = jnp.mean(o, axis=-1, keepdims=True)
                d = o - mu
                var = jnp.mean(d * d, axis=-1, keepdims=True)
                yn = d * lax.rsqrt(var + EPS) * gn_ref[:, lo:hi]
                y_ref[t0:t1, CONV_WIDTH + lo:CONV_WIDTH + hi] = (_silu(g) * yn).astype(BF16)
            new_states.append(st)
        return new_states

    def out_proj(r0, r1):
        return x_ref[0, r0:r1, :] + jnp.dot(y_ref[r0:r1, :], wout_ref[...],
                                            preferred_element_type=F32)

    def ffn_hidden(h1):
        u2 = _rmsnorm(h1, gffn_ref[...]).astype(BF16)
        gate = jnp.dot(u2, wg_ref[...], preferred_element_type=F32)
        up = jnp.dot(u2, wu_ref[...], preferred_element_type=F32)
        return (_silu(gate) * up).astype(BF16)

    def ffn_down(h1, hid):
        return h1 + jnp.dot(hid, wd_ref[...], preferred_element_type=F32)

    def ple(r0, r1, h2):
        u3 = _rmsnorm(h2, gple_ref[...]).astype(BF16)
        pgate = jax.nn.sigmoid(jnp.dot(u3, wpg_ref[...], preferred_element_type=F32))
        pproj = jnp.dot(p_ref[0, r0:r1, :].astype(BF16), wpp_ref[...],
                        preferred_element_type=F32)
        return h2 + pgate * pproj

    proj = [in_proj(r0, r1) for r0, r1 in subs]
    states = [state_ref[h] for h in range(RET_HEADS)]
    for (r0, r1), (pc, pr) in zip(subs, proj):
        conv_mixer(r0, r1, pc)
        states = retention(r0, r1, pr, states)
    for h in range(RET_HEADS):
        state_ref[h] = states[h]
    h1 = [out_proj(r0, r1) for r0, r1 in subs]
    hid = [ffn_hidden(a) for a in h1]
    h2 = [ffn_down(a, b) for a, b in zip(h1, hid)]
    h3 = [ple(r0, r1, a) for (r0, r1), a in zip(subs, h2)]
    for (r0, r1), a in zip(subs, h3):
        o_ref[0, r0:r1, :] = _rmsnorm(a, gfin_ref[...])


def _rope_tables(seq, k_scale):
    half = RET_HEAD_DIM // 2
    pos = jnp.arange(seq, dtype=F32)
    inv_freq = ROPE_BASE ** (-jnp.arange(half, dtype=F32) / half)
    ang = pos[:, None] * inv_freq[None, :]
    cos = jnp.cos(ang)
    sin = jnp.sin(ang)
    cos_full = jnp.concatenate([cos, cos], axis=-1)
    sin_full = jnp.concatenate([-sin, sin], axis=-1)
    return cos_full, sin_full, cos_full * k_scale, sin_full * k_scale


def _decay_tables(chunk):
    hh = jnp.arange(RET_HEADS, dtype=F32)
    log_gamma = jnp.log(1.0 - jnp.power(2.0, -5.0 - hh))
    idx = jnp.arange(chunk, dtype=F32)
    diff = idx[:, None] - idx[None, :]
    mask = jnp.where(diff[None] >= 0,
                     jnp.exp(log_gamma[:, None, None] * jnp.maximum(diff, 0.0)[None]),
                     0.0)
    q_decay = jnp.exp(log_gamma[:, None] * (idx + 1.0)[None])
    k_decay = jnp.exp(log_gamma[:, None] * (chunk - 1 - idx)[None])
    chunk_decay = jnp.exp(log_gamma * chunk)
    lanes = (RET_HEADS, chunk, RET_HEAD_DIM)
    qd = jnp.broadcast_to(q_decay[:, :, None], lanes)
    kd = jnp.broadcast_to(k_decay[:, :, None], lanes)
    cd = jnp.broadcast_to(chunk_decay[:, None, None], (RET_HEADS, 1, RET_HEAD_DIM))
    return mask, qd, kd, cd


def kernel(x, p, g_mix, w_in, conv_w, ret_gn, w_out, g_ffn, w_gate, w_up, w_down,
           g_ple, w_ple_gate, w_ple_proj, g_final):
    b, s, d = x.shape
    assert d == D_MODEL and s % SEQ_TILE == 0
    assert SEQ_TILE % SUB_ROWS == 0 and SUB_ROWS % RET_CHUNK == 0
    assert p.shape[0] == 1, "single-layer block"
    ts = SEQ_TILE

    cq, sq, ck, sk = _rope_tables(s, RET_HEAD_DIM ** -0.5)
    mask, qd, kd, cd = _decay_tables(RET_CHUNK)

    row = lambda a: a.reshape(1, -1).astype(F32)
    whole = pl.BlockSpec(memory_space=pltpu.VMEM)
    rope_spec = pl.BlockSpec((ts, RET_HEAD_DIM), lambda bi, si: (si, 0))

    operands = [
        (x, pl.BlockSpec((1, ts, D_MODEL), lambda bi, si: (bi, si, 0))),
        (p[0], pl.BlockSpec((1, ts, PLE_DIM), lambda bi, si: (bi, si, 0))),
        (cq, rope_spec), (sq, rope_spec), (ck, rope_spec), (sk, rope_spec),
        (mask, whole), (qd, whole), (kd, whole), (cd, whole),
        (row(g_mix[0]), whole), (w_in[0].astype(BF16), whole),
        (conv_w[0].astype(F32), whole), (row(ret_gn[0]), whole),
        (w_out[0].astype(BF16), whole),
        (row(g_ffn[0]), whole), (w_gate[0].astype(BF16), whole),
        (w_up[0].astype(BF16), whole), (w_down[0].astype(BF16), whole),
        (row(g_ple[0]), whole), (w_ple_gate[0].astype(BF16), whole),
        (w_ple_proj[0].astype(BF16), whole), (row(g_final), whole),
    ]
    args = [a for a, _ in operands]
    in_specs = [sp for _, sp in operands]

    return pl.pallas_call(
        _block_kernel,
        grid=(b, s // ts),
        in_specs=in_specs,
        out_specs=pl.BlockSpec((1, ts, D_MODEL), lambda bi, si: (bi, si, 0)),
        out_shape=jax.ShapeDtypeStruct((b, s, D_MODEL), x.dtype),
        scratch_shapes=[
            pltpu.VMEM((RET_HEADS, RET_HEAD_DIM, RET_HEAD_DIM), F32),
            pltpu.VMEM((CONV_WIDTH // LANES, HIST_ROWS + ts, LANES), F32),
            pltpu.VMEM((ts, MIX_WIDTH), BF16),
        ],
        compiler_params=pltpu.CompilerParams(
            dimension_semantics=("arbitrary", "arbitrary"),
            vmem_limit_bytes=VMEM_LIMIT_BYTES),
        name="hybrid_block",
    )(*args)
```

```python
import functools

import jax
import jax.numpy as jnp
from jax import lax
from jax.experimental import pallas as pl
from jax.experimental.pallas import tpu as pltpu

D_MODEL = 1024
PLE_DIM = 256
CONV_WIDTH = 512
CONV_K = 3
RET_HEADS = 4
RET_HEAD_DIM = 128
RET_WIDTH = RET_HEADS * RET_HEAD_DIM
MIX_WIDTH = CONV_WIDTH + RET_WIDTH
CONV_COLS = 3 * CONV_WIDTH
IN_COLS = CONV_COLS + 4 * RET_WIDTH
D_FF = 2816
ROPE_BASE = 10000.0
EPS = 1e-6

SEQ_TILE = 512
SUB_ROWS = 256
RET_CHUNK = 256
HIST_ROWS = 8
LANES = 128
VMEM_LIMIT_BYTES = 60 * 1024 * 1024

F32 = jnp.float32
BF16 = jnp.bfloat16


def _rmsnorm(x, g):
    return x * lax.rsqrt(jnp.mean(x * x, axis=-1, keepdims=True) + EPS) * g


def _silu(x):
    return x * jax.nn.sigmoid(x)


def _block_kernel(n_seq_tiles, n_tiles,
                  x_ref, p_ref, cq_ref, sq_ref, ck_ref, sk_ref,
                  mask_ref, qd_ref, kd_ref, cd_ref,
                  gmix_ref, win_ref, convw_ref, gn_ref, wout_ref,
                  gffn_ref, wg_ref, wu_ref, wd_ref,
                  gple_ref, wpg_ref, wpp_ref, gfin_ref,
                  o_ref, state_ref, zbuf_ref, y_ref, h2_ref, u3_ref):
    ts = SEQ_TILE
    t = pl.program_id(0)
    seq_start = lax.rem(t, n_seq_tiles) == 0

    @pl.when(seq_start)
    def _():
        state_ref[...] = jnp.zeros_like(state_ref)
        zbuf_ref[:, 0:HIST_ROWS, :] = jnp.zeros((CONV_WIDTH // LANES, HIST_ROWS, LANES), F32)

    @pl.when(jnp.logical_not(seq_start))
    def _():
        zbuf_ref[:, 0:HIST_ROWS, :] = zbuf_ref[:, ts:ts + HIST_ROWS, :]

    subs = [(i * SUB_ROWS, (i + 1) * SUB_ROWS) for i in range(ts // SUB_ROWS)]
    half = RET_HEAD_DIM // 2

    def in_proj(r0, r1):
        x = x_ref[0, r0:r1, :]
        u = _rmsnorm(x, gmix_ref[...]).astype(BF16)
        pc = jnp.dot(u, win_ref[:, 0:CONV_COLS], preferred_element_type=F32)
        pr = jnp.dot(u, win_ref[:, CONV_COLS:IN_COLS], preferred_element_type=F32)
        return pc, pr

    def conv_mixer(r0, r1, pc):
        for c in range(CONV_WIDTH // LANES):
            lo = c * LANES
            hi = lo + LANES
            z = pc[:, CONV_WIDTH + lo:CONV_WIDTH + hi] * pc[:, 2 * CONV_WIDTH + lo:2 * CONV_WIDTH + hi]
            zbuf_ref[c, HIST_ROWS + r0:HIST_ROWS + r1, :] = z
            z1 = zbuf_ref[c, HIST_ROWS - 1 + r0:HIST_ROWS - 1 + r1, :]
            z2 = zbuf_ref[c, HIST_ROWS - 2 + r0:HIST_ROWS - 2 + r1, :]
            conv = (convw_ref[0:1, lo:hi] * z2 + convw_ref[1:2, lo:hi] * z1
                    + convw_ref[2:3, lo:hi] * z)
            y_ref[r0:r1, lo:hi] = (pc[:, lo:hi] * conv).astype(BF16)

    def retention(r0, r1, pr, states):
        new_states = []
        for h in range(RET_HEADS):
            lo = h * RET_HEAD_DIM
            hi = lo + RET_HEAD_DIM
            st = states[h]
            for c0 in range(0, r1 - r0, RET_CHUNK):
                c1 = c0 + RET_CHUNK
                q = pr[c0:c1, lo:hi]
                k = pr[c0:c1, RET_WIDTH + lo:RET_WIDTH + hi]
                v = pr[c0:c1, 2 * RET_WIDTH + lo:2 * RET_WIDTH + hi]
                g = pr[c0:c1, 3 * RET_WIDTH + lo:3 * RET_WIDTH + hi]
                t0, t1 = r0 + c0, r0 + c1
                qr = q * cq_ref[t0:t1, :] + pltpu.roll(q, half, 1) * sq_ref[t0:t1, :]
                kr = k * ck_ref[t0:t1, :] + pltpu.roll(k, half, 1) * sk_ref[t0:t1, :]
                qb = qr.astype(BF16)
                kb = kr.astype(BF16)
                vb = v.astype(BF16)
                scores = lax.dot_general(qb, kb, (((1,), (1,)), ((), ())),
                                         preferred_element_type=F32)
                intra = jnp.dot((scores * mask_ref[h]).astype(BF16), vb,
                                preferred_element_type=F32)
                cross = jnp.dot((qr * qd_ref[h]).astype(BF16), st.astype(BF16),
                                preferred_element_type=F32)
                o = intra + cross
                kv = lax.dot_general((kr * kd_ref[h]).astype(BF16), vb,
                                     (((0,), (0,)), ((), ())), preferred_element_type=F32)
                st = st * cd_ref[h] + kv
                mu = jnp.mean(o, axis=-1, keepdims=True)
                d = o - mu
                var = jnp.mean(d * d, axis=-1, keepdims=True)
                yn = d * lax.rsqrt(var + EPS) * gn_ref[:, lo:hi]
                y_ref[t0:t1, CONV_WIDTH + lo:CONV_WIDTH + hi] = (_silu(g) * yn).astype(BF16)
            new_states.append(st)
        return new_states

    def out_proj(r0, r1):
        return x_ref[0, r0:r1, :] + jnp.dot(y_ref[r0:r1, :], wout_ref[...],
                                            preferred_element_type=F32)

    def ffn_hidden(h1):
        u2 = _rmsnorm(h1, gffn_ref[...]).astype(BF16)
        gate = jnp.dot(u2, wg_ref[...], preferred_element_type=F32)
        up = jnp.dot(u2, wu_ref[...], preferred_element_type=F32)
        return (_silu(gate) * up).astype(BF16)

    def ffn_down(h1, hid):
        return h1 + jnp.dot(hid, wd_ref[...], preferred_element_type=F32)

    def ple_input(i, r0, r1):
        if i == 0:
            return u3_ref[...]
        return _rmsnorm(h2_ref[r0:r1, :], gple_ref[...]).astype(BF16)

    def ple_and_final_norm(r0, r1, u3):
        pgate = jax.nn.sigmoid(jnp.dot(u3, wpg_ref[...], preferred_element_type=F32))
        pproj = jnp.dot(p_ref[0, r0:r1, :].astype(BF16), wpp_ref[...],
                        preferred_element_type=F32)
        h3 = h2_ref[r0:r1, :] + pgate * pproj
        o_ref[0, r0:r1, :] = _rmsnorm(h3, gfin_ref[...])

    def program(body, finish):
        if finish:
            u3 = [ple_input(i, r0, r1) for i, (r0, r1) in enumerate(subs)]
        proj = []
        for i, (r0, r1) in enumerate(subs):
            if finish:
                ple_and_final_norm(r0, r1, u3[i])
            if body:
                proj.append(in_proj(r0, r1))
        if not body:
            return
        states = [state_ref[h] for h in range(RET_HEADS)]
        for (r0, r1), (pc, pr) in zip(subs, proj):
            conv_mixer(r0, r1, pc)
            states = retention(r0, r1, pr, states)
        for h in range(RET_HEADS):
            state_ref[h] = states[h]
        h1 = [out_proj(r0, r1) for r0, r1 in subs]
        hid = [ffn_hidden(a) for a in h1]
        for i, (r0, r1) in enumerate(subs):
            h2 = ffn_down(h1[i], hid[i])
            h2_ref[r0:r1, :] = h2
            if i == 0:
                u3_ref[...] = _rmsnorm(h2, gple_ref[...]).astype(BF16)

    @pl.when(t == 0)
    def _():
        program(body=True, finish=False)

    @pl.when(jnp.logical_and(t > 0, t < n_tiles))
    def _():
        program(body=True, finish=True)

    @pl.when(t == n_tiles)
    def _():
        program(body=False, finish=True)


def _rope_tables(seq, k_scale):
    half = RET_HEAD_DIM // 2
    pos = jnp.arange(seq, dtype=F32)
    inv_freq = ROPE_BASE ** (-jnp.arange(half, dtype=F32) / half)
    ang = pos[:, None] * inv_freq[None, :]
    cos = jnp.cos(ang)
    sin = jnp.sin(ang)
    cos_full = jnp.concatenate([cos, cos], axis=-1)
    sin_full = jnp.concatenate([-sin, sin], axis=-1)
    return cos_full, sin_full, cos_full * k_scale, sin_full * k_scale


def _decay_tables(chunk):
    hh = jnp.arange(RET_HEADS, dtype=F32)
    log_gamma = jnp.log(1.0 - jnp.power(2.0, -5.0 - hh))
    idx = jnp.arange(chunk, dtype=F32)
    diff = idx[:, None] - idx[None, :]
    mask = jnp.where(diff[None] >= 0,
                     jnp.exp(log_gamma[:, None, None] * jnp.maximum(diff, 0.0)[None]),
                     0.0)
    q_decay = jnp.exp(log_gamma[:, None] * (idx + 1.0)[None])
    k_decay = jnp.exp(log_gamma[:, None] * (chunk - 1 - idx)[None])
    chunk_decay = jnp.exp(log_gamma * chunk)
    lanes = (RET_HEADS, chunk, RET_HEAD_DIM)
    qd = jnp.broadcast_to(q_decay[:, :, None], lanes)
    kd = jnp.broadcast_to(k_decay[:, :, None], lanes)
    cd = jnp.broadcast_to(chunk_decay[:, None, None], (RET_HEADS, 1, RET_HEAD_DIM))
    return mask, qd, kd, cd


def kernel(x, p, g_mix, w_in, conv_w, ret_gn, w_out, g_ffn, w_gate, w_up, w_down,
           g_ple, w_ple_gate, w_ple_proj, g_final):
    b, s, d = x.shape
    assert d == D_MODEL and s % SEQ_TILE == 0
    assert SEQ_TILE % SUB_ROWS == 0 and SUB_ROWS % RET_CHUNK == 0
    assert p.shape[0] == 1, "single-layer block"
    ts = SEQ_TILE
    ns = s // ts
    n_tiles = b * ns

    cq, sq, ck, sk = _rope_tables(s, RET_HEAD_DIM ** -0.5)
    mask, qd, kd, cd = _decay_tables(RET_CHUNK)

    def body_tile(t):
        tb = jnp.minimum(t, n_tiles - 1)
        return tb // ns, tb % ns

    def finish_tile(t):
        tf = jnp.maximum(t - 1, 0)
        return tf // ns, tf % ns

    row = lambda a: a.reshape(1, -1).astype(F32)
    whole = pl.BlockSpec(memory_space=pltpu.VMEM)
    rope_spec = pl.BlockSpec((ts, RET_HEAD_DIM), lambda t: (body_tile(t)[1], 0))

    operands = [
        (x, pl.BlockSpec((1, ts, D_MODEL), lambda t: (*body_tile(t), 0))),
        (p[0], pl.BlockSpec((1, ts, PLE_DIM), lambda t: (*finish_tile(t), 0))),
        (cq, rope_spec), (sq, rope_spec), (ck, rope_spec), (sk, rope_spec),
        (mask, whole), (qd, whole), (kd, whole), (cd, whole),
        (row(g_mix[0]), whole), (w_in[0].astype(BF16), whole),
        (conv_w[0].astype(F32), whole), (row(ret_gn[0]), whole),
        (w_out[0].astype(BF16), whole),
        (row(g_ffn[0]), whole), (w_gate[0].astype(BF16), whole),
        (w_up[0].astype(BF16), whole), (w_down[0].astype(BF16), whole),
        (row(g_ple[0]), whole), (w_ple_gate[0].astype(BF16), whole),
        (w_ple_proj[0].astype(BF16), whole), (row(g_final), whole),
    ]
    args = [a for a, _ in operands]
    in_specs = [sp for _, sp in operands]

    return pl.pallas_call(
        functools.partial(_block_kernel, ns, n_tiles),
        grid=(n_tiles + 1,),
        in_specs=in_specs,
        out_specs=pl.BlockSpec((1, ts, D_MODEL), lambda t: (*finish_tile(t), 0)),
        out_shape=jax.ShapeDtypeStruct((b, s, D_MODEL), x.dtype),
        scratch_shapes=[
            pltpu.VMEM((RET_HEADS, RET_HEAD_DIM, RET_HEAD_DIM), F32),
            pltpu.VMEM((CONV_WIDTH // LANES, HIST_ROWS + ts, LANES), F32),
            pltpu.VMEM((ts, MIX_WIDTH), BF16),
            pltpu.VMEM((ts, D_MODEL), F32),
            pltpu.VMEM((SUB_ROWS, D_MODEL), BF16),
        ],
        compiler_params=pltpu.CompilerParams(
            dimension_semantics=("arbitrary",),
            vmem_limit_bytes=VMEM_LIMIT_BYTES),
        name="hybrid_block",
    )(*args)
```

```python
import jax
import jax.numpy as jnp
import numpy as np
from jax import lax
from jax.experimental import pallas as pl
from jax.experimental.pallas import tpu as pltpu

D_MODEL = 1024
PLE_DIM = 256
CONV_WIDTH = 512
CONV_K = 3
RET_HEADS = 4
RET_HEAD_DIM = 128
RET_WIDTH = RET_HEADS * RET_HEAD_DIM
MIX_WIDTH = CONV_WIDTH + RET_WIDTH
CONV_COLS = 3 * CONV_WIDTH
IN_COLS = CONV_COLS + 4 * RET_WIDTH
D_FF = 2816
ROPE_BASE = 10000.0
EPS = 1e-6

SEQ_TILE = 512
SUB_ROWS = 256
RET_CHUNK = 256
HIST_ROWS = 8
LANES = 128
VMEM_LIMIT_BYTES = 60 * 1024 * 1024

F32 = jnp.float32
BF16 = jnp.bfloat16


def _rmsnorm(x, g):
    return x * lax.rsqrt(jnp.mean(x * x, axis=-1, keepdims=True) + EPS) * g


def _silu(x):
    return x * jax.nn.sigmoid(x)


def _block_kernel(x_ref, p_ref, cq_ref, sq_ref, ck_ref, sk_ref,
                  mask_ref, qd_ref, kd_ref, cd_ref,
                  gmix_ref, win_ref, convw_ref, gn_ref, wout_ref,
                  gffn_ref, wg_ref, wu_ref, wd_ref,
                  gple_ref, wpg_ref, wpp_ref, gfin_ref,
                  o_ref, state_ref, zbuf_ref, y_ref):
    ts = SEQ_TILE
    s = pl.program_id(1)

    @pl.when(s == 0)
    def _():
        state_ref[...] = jnp.zeros_like(state_ref)
        zbuf_ref[:, 0:HIST_ROWS, :] = jnp.zeros((CONV_WIDTH // LANES, HIST_ROWS, LANES), F32)

    @pl.when(s > 0)
    def _():
        zbuf_ref[:, 0:HIST_ROWS, :] = zbuf_ref[:, ts:ts + HIST_ROWS, :]

    subs = [(i * SUB_ROWS, (i + 1) * SUB_ROWS) for i in range(ts // SUB_ROWS)]
    half = RET_HEAD_DIM // 2

    def in_proj(r0, r1):
        x = x_ref[0, r0:r1, :]
        u = _rmsnorm(x, gmix_ref[...]).astype(BF16)
        pc = jnp.dot(u, win_ref[:, 0:CONV_COLS], preferred_element_type=F32)
        pr = jnp.dot(u, win_ref[:, CONV_COLS:IN_COLS], preferred_element_type=F32)
        return pc, pr

    def conv_mixer(r0, r1, pc):
        for c in range(CONV_WIDTH // LANES):
            lo = c * LANES
            hi = lo + LANES
            z = pc[:, CONV_WIDTH + lo:CONV_WIDTH + hi] * pc[:, 2 * CONV_WIDTH + lo:2 * CONV_WIDTH + hi]
            zbuf_ref[c, HIST_ROWS + r0:HIST_ROWS + r1, :] = z
            z1 = zbuf_ref[c, HIST_ROWS - 1 + r0:HIST_ROWS - 1 + r1, :]
            z2 = zbuf_ref[c, HIST_ROWS - 2 + r0:HIST_ROWS - 2 + r1, :]
            conv = (convw_ref[0:1, lo:hi] * z2 + convw_ref[1:2, lo:hi] * z1
                    + convw_ref[2:3, lo:hi] * z)
            y_ref[r0:r1, lo:hi] = (pc[:, lo:hi] * conv).astype(BF16)

    def retention(r0, r1, pr, states):
        new_states = []
        for h in range(RET_HEADS):
            lo = h * RET_HEAD_DIM
            hi = lo + RET_HEAD_DIM
            st = states[h]
            for c0 in range(0, r1 - r0, RET_CHUNK):
                c1 = c0 + RET_CHUNK
                q = pr[c0:c1, lo:hi]
                k = pr[c0:c1, RET_WIDTH + lo:RET_WIDTH + hi]
                v = pr[c0:c1, 2 * RET_WIDTH + lo:2 * RET_WIDTH + hi]
                g = pr[c0:c1, 3 * RET_WIDTH + lo:3 * RET_WIDTH + hi]
                t0, t1 = r0 + c0, r0 + c1
                qr = q * cq_ref[t0:t1, :] + pltpu.roll(q, half, 1) * sq_ref[t0:t1, :]
                kr = k * ck_ref[t0:t1, :] + pltpu.roll(k, half, 1) * sk_ref[t0:t1, :]
                qb = qr.astype(BF16)
                kb = kr.astype(BF16)
                vb = v.astype(BF16)
                scores = lax.dot_general(qb, kb, (((1,), (1,)), ((), ())),
                                         preferred_element_type=F32)
                intra = jnp.dot((scores * mask_ref[h]).astype(BF16), vb,
                                preferred_element_type=F32)
                cross = jnp.dot((qr * qd_ref[h]).astype(BF16), st.astype(BF16),
                                preferred_element_type=F32)
                o = intra + cross
                kv = lax.dot_general((kr * kd_ref[h]).astype(BF16), vb,
                                     (((0,), (0,)), ((), ())), preferred_element_type=F32)
                st = st * cd_ref[h] + kv
                mu = jnp.mean(o, axis=-1, keepdims=True)
                d = o - mu
                var = jnp.mean(d * d, axis=-1, keepdims=True)
                yn = d * lax.rsqrt(var + EPS) * gn_ref[:, lo:hi]
                y_ref[t0:t1, CONV_WIDTH + lo:CONV_WIDTH + hi] = (_silu(g) * yn).astype(BF16)
            new_states.append(st)
        return new_states

    def out_proj(r0, r1):
        return x_ref[0, r0:r1, :] + jnp.dot(y_ref[r0:r1, :], wout_ref[...],
                                            preferred_element_type=F32)

    def ffn_hidden(h1):
        u2 = _rmsnorm(h1, gffn_ref[...]).astype(BF16)
        gate = jnp.dot(u2, wg_ref[...], preferred_element_type=F32)
        up = jnp.dot(u2, wu_ref[...], preferred_element_type=F32)
        return (_silu(gate) * up).astype(BF16)

    def ffn_down(h1, hid):
        return h1 + jnp.dot(hid, wd_ref[...], preferred_element_type=F32)

    def ple(r0, r1, h2):
        u3 = _rmsnorm(h2, gple_ref[...]).astype(BF16)
        pgate = jax.nn.sigmoid(jnp.dot(u3, wpg_ref[...], preferred_element_type=F32))
        pproj = jnp.dot(p_ref[0, r0:r1, :].astype(BF16), wpp_ref[...],
                        preferred_element_type=F32)
        return h2 + pgate * pproj

    proj = [in_proj(r0, r1) for r0, r1 in subs]
    states = [state_ref[h] for h in range(RET_HEADS)]
    for (r0, r1), (pc, pr) in zip(subs, proj):
        conv_mixer(r0, r1, pc)
        states = retention(r0, r1, pr, states)
    for h in range(RET_HEADS):
        state_ref[h] = states[h]
    h1 = [out_proj(r0, r1) for r0, r1 in subs]
    hid = [ffn_hidden(a) for a in h1]
    h2 = [ffn_down(a, b) for a, b in zip(h1, hid)]
    h3 = [ple(r0, r1, a) for (r0, r1), a in zip(subs, h2)]
    for (r0, r1), a in zip(subs, h3):
        o_ref[0, r0:r1, :] = _rmsnorm(a, gfin_ref[...])


def _rope_tables(seq, k_scale):
    half = RET_HEAD_DIM // 2
    pos = np.arange(seq, dtype=np.float64)
    inv_freq = ROPE_BASE ** (-np.arange(half, dtype=np.float64) / half)
    ang = pos[:, None] * inv_freq[None, :]
    cos_full = np.concatenate([np.cos(ang), np.cos(ang)], axis=-1)
    sin_full = np.concatenate([-np.sin(ang), np.sin(ang)], axis=-1)
    tables = (cos_full, sin_full, cos_full * k_scale, sin_full * k_scale)
    return tuple(t.astype(np.float32) for t in tables)


def _decay_tables(chunk):
    log_gamma = np.log(1.0 - np.power(2.0, -5.0 - np.arange(RET_HEADS, dtype=np.float64)))
    idx = np.arange(chunk, dtype=np.float64)
    diff = idx[:, None] - idx[None, :]
    mask = np.where(diff[None] >= 0,
                    np.exp(log_gamma[:, None, None] * np.maximum(diff, 0.0)[None]), 0.0)
    q_decay = np.exp(log_gamma[:, None] * (idx + 1.0)[None])
    k_decay = np.exp(log_gamma[:, None] * (chunk - 1 - idx)[None])
    chunk_decay = np.exp(log_gamma * chunk)
    lanes = (RET_HEADS, chunk, RET_HEAD_DIM)
    qd = np.broadcast_to(q_decay[:, :, None], lanes)
    kd = np.broadcast_to(k_decay[:, :, None], lanes)
    cd = np.broadcast_to(chunk_decay[:, None, None], (RET_HEADS, 1, RET_HEAD_DIM))
    return tuple(np.ascontiguousarray(t, dtype=np.float32) for t in (mask, qd, kd, cd))


def kernel(x, p, g_mix, w_in, conv_w, ret_gn, w_out, g_ffn, w_gate, w_up, w_down,
           g_ple, w_ple_gate, w_ple_proj, g_final):
    b, s, d = x.shape
    assert d == D_MODEL and s % SEQ_TILE == 0
    assert SEQ_TILE % SUB_ROWS == 0 and SUB_ROWS % RET_CHUNK == 0
    assert p.shape[0] == 1, "single-layer block"
    ts = SEQ_TILE

    cq, sq, ck, sk = _rope_tables(s, RET_HEAD_DIM ** -0.5)
    mask, qd, kd, cd = _decay_tables(RET_CHUNK)

    row = lambda a: a.reshape(1, -1).astype(F32)
    whole = pl.BlockSpec(memory_space=pltpu.VMEM)
    rope_spec = pl.BlockSpec((ts, RET_HEAD_DIM), lambda bi, si: (si, 0))

    operands = [
        (x, pl.BlockSpec((1, ts, D_MODEL), lambda bi, si: (bi, si, 0))),
        (p[0], pl.BlockSpec((1, ts, PLE_DIM), lambda bi, si: (bi, si, 0))),
        (cq, rope_spec), (sq, rope_spec), (ck, rope_spec), (sk, rope_spec),
        (mask, whole), (qd, whole), (kd, whole), (cd, whole),
        (row(g_mix[0]), whole), (w_in[0].astype(BF16), whole),
        (conv_w[0].astype(F32), whole), (row(ret_gn[0]), whole),
        (w_out[0].astype(BF16), whole),
        (row(g_ffn[0]), whole), (w_gate[0].astype(BF16), whole),
        (w_up[0].astype(BF16), whole), (w_down[0].astype(BF16), whole),
        (row(g_ple[0]), whole), (w_ple_gate[0].astype(BF16), whole),
        (w_ple_proj[0].astype(BF16), whole), (row(g_final), whole),
    ]
    args = [a for a, _ in operands]
    in_specs = [sp for _, sp in operands]

    return pl.pallas_call(
        _block_kernel,
        grid=(b, s // ts),
        in_specs=in_specs,
        out_specs=pl.BlockSpec((1, ts, D_MODEL), lambda bi, si: (bi, si, 0)),
        out_shape=jax.ShapeDtypeStruct((b, s, D_MODEL), x.dtype),
        scratch_shapes=[
            pltpu.VMEM((RET_HEADS, RET_HEAD_DIM, RET_HEAD_DIM), F32),
            pltpu.VMEM((CONV_WIDTH // LANES, HIST_ROWS + ts, LANES), F32),
            pltpu.VMEM((ts, MIX_WIDTH), BF16),
        ],
        compiler_params=pltpu.CompilerParams(
            dimension_semantics=("arbitrary", "arbitrary"),
            vmem_limit_bytes=VMEM_LIMIT_BYTES),
        name="hybrid_block",
    )(*args)
```

```python
import jax
import jax.numpy as jnp
import numpy as np
from jax import lax
from jax.experimental import pallas as pl
from jax.experimental.pallas import tpu as pltpu

D_MODEL = 1024
PLE_DIM = 256
CONV_WIDTH = 512
CONV_K = 3
RET_HEADS = 4
RET_HEAD_DIM = 128
RET_WIDTH = RET_HEADS * RET_HEAD_DIM
MIX_WIDTH = CONV_WIDTH + RET_WIDTH
CONV_COLS = 3 * CONV_WIDTH
IN_COLS = CONV_COLS + 4 * RET_WIDTH
D_FF = 2816
ROPE_BASE = 10000.0
EPS = 1e-6

SEQ_TILE = 512
SUB_ROWS = 256
RET_CHUNK = 256
HIST_ROWS = 8
LANES = 128
VMEM_LIMIT_BYTES = 60 * 1024 * 1024

F32 = jnp.float32
BF16 = jnp.bfloat16


def _rmsnorm(x, g):
    return x * lax.rsqrt(jnp.mean(x * x, axis=-1, keepdims=True) + EPS) * g


def _silu(x):
    return x * jax.nn.sigmoid(x)


def _block_kernel(x_ref, p_ref, cq_ref, sq_ref, ck_ref, sk_ref,
                  mask_ref, qd_ref, kd_ref, cd_ref,
                  gmix_ref, win_ref, convw_ref, gn_ref, wout_ref,
                  gffn_ref, wg_ref, wu_ref, wd_ref,
                  gple_ref, wpg_ref, wpp_ref, gfin_ref,
                  o_ref, state_ref, zbuf_ref, y_ref):
    ts = SEQ_TILE
    s = pl.program_id(1)

    @pl.when(s == 0)
    def _():
        state_ref[...] = jnp.zeros_like(state_ref)
        zbuf_ref[:, 0:HIST_ROWS, :] = jnp.zeros((CONV_WIDTH // LANES, HIST_ROWS, LANES), F32)

    @pl.when(s > 0)
    def _():
        zbuf_ref[:, 0:HIST_ROWS, :] = zbuf_ref[:, ts:ts + HIST_ROWS, :]

    subs = [(i * SUB_ROWS, (i + 1) * SUB_ROWS) for i in range(ts // SUB_ROWS)]
    half = RET_HEAD_DIM // 2

    def in_proj(r0, r1):
        x = x_ref[0, r0:r1, :]
        u = _rmsnorm(x, gmix_ref[...]).astype(BF16)
        pc = jnp.dot(u, win_ref[:, 0:CONV_COLS], preferred_element_type=F32)
        pr = jnp.dot(u, win_ref[:, CONV_COLS:IN_COLS], preferred_element_type=F32)
        return pc, pr

    def conv_mixer(r0, r1, pc):
        for c in range(CONV_WIDTH // LANES):
            lo = c * LANES
            hi = lo + LANES
            z = pc[:, CONV_WIDTH + lo:CONV_WIDTH + hi] * pc[:, 2 * CONV_WIDTH + lo:2 * CONV_WIDTH + hi]
            zbuf_ref[c, HIST_ROWS + r0:HIST_ROWS + r1, :] = z
            z1 = zbuf_ref[c, HIST_ROWS - 1 + r0:HIST_ROWS - 1 + r1, :]
            z2 = zbuf_ref[c, HIST_ROWS - 2 + r0:HIST_ROWS - 2 + r1, :]
            conv = (convw_ref[0:1, lo:hi] * z2 + convw_ref[1:2, lo:hi] * z1
                    + convw_ref[2:3, lo:hi] * z)
            y_ref[r0:r1, lo:hi] = (pc[:, lo:hi] * conv).astype(BF16)

    def retention(r0, r1, pr, states):
        states = list(states)
        for c0 in range(0, r1 - r0, RET_CHUNK):
            c1 = c0 + RET_CHUNK
            t0, t1 = r0 + c0, r0 + c1
            scores, cross, vbs = [], [], []
            for h in range(RET_HEADS):
                lo = h * RET_HEAD_DIM
                hi = lo + RET_HEAD_DIM
                q = pr[c0:c1, lo:hi]
                k = pr[c0:c1, RET_WIDTH + lo:RET_WIDTH + hi]
                v = pr[c0:c1, 2 * RET_WIDTH + lo:2 * RET_WIDTH + hi]
                qr = q * cq_ref[t0:t1, :] + pltpu.roll(q, half, 1) * sq_ref[t0:t1, :]
                kr = k * ck_ref[t0:t1, :] + pltpu.roll(k, half, 1) * sk_ref[t0:t1, :]
                qb = qr.astype(BF16)
                kb = kr.astype(BF16)
                vb = v.astype(BF16)
                st = states[h]
                scores.append(lax.dot_general(qb, kb, (((1,), (1,)), ((), ())),
                                              preferred_element_type=F32))
                cross.append(jnp.dot((qr * qd_ref[h]).astype(BF16), st.astype(BF16),
                                     preferred_element_type=F32))
                kv = lax.dot_general((kr * kd_ref[h]).astype(BF16), vb,
                                     (((0,), (0,)), ((), ())), preferred_element_type=F32)
                states[h] = st * cd_ref[h] + kv
                vbs.append(vb)
            for h in range(RET_HEADS):
                lo = h * RET_HEAD_DIM
                hi = lo + RET_HEAD_DIM
                g = pr[c0:c1, 3 * RET_WIDTH + lo:3 * RET_WIDTH + hi]
                intra = jnp.dot((scores[h] * mask_ref[h]).astype(BF16), vbs[h],
                                preferred_element_type=F32)
                o = intra + cross[h]
                mu = jnp.mean(o, axis=-1, keepdims=True)
                d = o - mu
                var = jnp.mean(d * d, axis=-1, keepdims=True)
                yn = d * lax.rsqrt(var + EPS) * gn_ref[:, lo:hi]
                y_ref[t0:t1, CONV_WIDTH + lo:CONV_WIDTH + hi] = (_silu(g) * yn).astype(BF16)
        return states

    def out_proj(r0, r1):
        return x_ref[0, r0:r1, :] + jnp.dot(y_ref[r0:r1, :], wout_ref[...],
                                            preferred_element_type=F32)

    def ffn_hidden(h1):
        u2 = _rmsnorm(h1, gffn_ref[...]).astype(BF16)
        gate = jnp.dot(u2, wg_ref[...], preferred_element_type=F32)
        up = jnp.dot(u2, wu_ref[...], preferred_element_type=F32)
        return (_silu(gate) * up).astype(BF16)

    def ffn_down(h1, hid):
        return h1 + jnp.dot(hid, wd_ref[...], preferred_element_type=F32)

    def ple(r0, r1, h2):
        u3 = _rmsnorm(h2, gple_ref[...]).astype(BF16)
        pgate = jax.nn.sigmoid(jnp.dot(u3, wpg_ref[...], preferred_element_type=F32))
        pproj = jnp.dot(p_ref[0, r0:r1, :].astype(BF16), wpp_ref[...],
                        preferred_element_type=F32)
        return h2 + pgate * pproj

    proj = [in_proj(r0, r1) for r0, r1 in subs]
    states = [state_ref[h] for h in range(RET_HEADS)]
    for (r0, r1), (pc, pr) in zip(subs, proj):
        conv_mixer(r0, r1, pc)
        states = retention(r0, r1, pr, states)
    for h in range(RET_HEADS):
        state_ref[h] = states[h]
    h1 = [out_proj(r0, r1) for r0, r1 in subs]
    hid = [ffn_hidden(a) for a in h1]
    h2 = [ffn_down(a, b) for a, b in zip(h1, hid)]
    h3 = [ple(r0, r1, a) for (r0, r1), a in zip(subs, h2)]
    for (r0, r1), a in zip(subs, h3):
        o_ref[0, r0:r1, :] = _rmsnorm(a, gfin_ref[...])


def _rope_tables(seq, k_scale):
    half = RET_HEAD_DIM // 2
    pos = np.arange(seq, dtype=np.float64)
    inv_freq = ROPE_BASE ** (-np.arange(half, dtype=np.float64) / half)
    ang = pos[:, None] * inv_freq[None, :]
    cos_full = np.concatenate([np.cos(ang), np.cos(ang)], axis=-1)
    sin_full = np.concatenate([-np.sin(ang), np.sin(ang)], axis=-1)
    tables = (cos_full, sin_full, cos_full * k_scale, sin_full * k_scale)
    return tuple(t.astype(np.float32) for t in tables)


def _decay_tables(chunk):
    log_gamma = np.log(1.0 - np.power(2.0, -5.0 - np.arange(RET_HEADS, dtype=np.float64)))
    idx = np.arange(chunk, dtype=np.float64)
    diff = idx[:, None] - idx[None, :]
    mask = np.where(diff[None] >= 0,
                    np.exp(log_gamma[:, None, None] * np.maximum(diff, 0.0)[None]), 0.0)
    q_decay = np.exp(log_gamma[:, None] * (idx + 1.0)[None])
    k_decay = np.exp(log_gamma[:, None] * (chunk - 1 - idx)[None])
    chunk_decay = np.exp(log_gamma * chunk)
    lanes = (RET_HEADS, chunk, RET_HEAD_DIM)
    qd = np.broadcast_to(q_decay[:, :, None], lanes)
    kd = np.broadcast_to(k_decay[:, :, None], lanes)
    cd = np.broadcast_to(chunk_decay[:, None, None], (RET_HEADS, 1, RET_HEAD_DIM))
    return tuple(np.ascontiguousarray(t, dtype=np.float32) for t in (mask, qd, kd, cd))


def kernel(x, p, g_mix, w_in, conv_w, ret_gn, w_out, g_ffn, w_gate, w_up, w_down,
           g_ple, w_ple_gate, w_ple_proj, g_final):
    b, s, d = x.shape
    assert d == D_MODEL and s % SEQ_TILE == 0
    assert SEQ_TILE % SUB_ROWS == 0 and SUB_ROWS % RET_CHUNK == 0
    assert p.shape[0] == 1, "single-layer block"
    ts = SEQ_TILE

    cq, sq, ck, sk = _rope_tables(s, RET_HEAD_DIM ** -0.5)
    mask, qd, kd, cd = _decay_tables(RET_CHUNK)

    row = lambda a: a.reshape(1, -1).astype(F32)
    whole = pl.BlockSpec(memory_space=pltpu.VMEM)
    rope_spec = pl.BlockSpec((ts, RET_HEAD_DIM), lambda bi, si: (si, 0))

    operands = [
        (x, pl.BlockSpec((1, ts, D_MODEL), lambda bi, si: (bi, si, 0))),
        (p[0], pl.BlockSpec((1, ts, PLE_DIM), lambda bi, si: (bi, si, 0))),
        (cq, rope_spec), (sq, rope_spec), (ck, rope_spec), (sk, rope_spec),
        (mask, whole), (qd, whole), (kd, whole), (cd, whole),
        (row(g_mix[0]), whole), (w_in[0].astype(BF16), whole),
        (conv_w[0].astype(F32), whole), (row(ret_gn[0]), whole),
        (w_out[0].astype(BF16), whole),
        (row(g_ffn[0]), whole), (w_gate[0].astype(BF16), whole),
        (w_up[0].astype(BF16), whole), (w_down[0].astype(BF16), whole),
        (row(g_ple[0]), whole), (w_ple_gate[0].astype(BF16), whole),
        (w_ple_proj[0].astype(BF16), whole), (row(g_final), whole),
    ]
    args = [a for a, _ in operands]
    in_specs = [sp for _, sp in operands]

    return pl.pallas_call(
        _block_kernel,
        grid=(b, s // ts),
        in_specs=in_specs,
        out_specs=pl.BlockSpec((1, ts, D_MODEL), lambda bi, si: (bi, si, 0)),
        out_shape=jax.ShapeDtypeStruct((b, s, D_MODEL), x.dtype),
        scratch_shapes=[
            pltpu.VMEM((RET_HEADS, RET_HEAD_DIM, RET_HEAD_DIM), F32),
            pltpu.VMEM((CONV_WIDTH // LANES, HIST_ROWS + ts, LANES), F32),
            pltpu.VMEM((ts, MIX_WIDTH), BF16),
        ],
        compiler_params=pltpu.CompilerParams(
            dimension_semantics=("arbitrary", "arbitrary"),
            vmem_limit_bytes=VMEM_LIMIT_BYTES),
        name="hybrid_block",
    )(*args)
```

```python
import jax
import jax.numpy as jnp
import numpy as np
from jax import lax
from jax.experimental import pallas as pl
from jax.experimental.pallas import tpu as pltpu

D_MODEL = 1024
PLE_DIM = 256
CONV_WIDTH = 512
CONV_K = 3
RET_HEADS = 4
RET_HEAD_DIM = 128
RET_WIDTH = RET_HEADS * RET_HEAD_DIM
MIX_WIDTH = CONV_WIDTH + RET_WIDTH
CONV_COLS = 3 * CONV_WIDTH
IN_COLS = CONV_COLS + 4 * RET_WIDTH
D_FF = 2816
ROPE_BASE = 10000.0
EPS = 1e-6

SEQ_TILE = 512
SUB_ROWS = 256
RET_CHUNK = 256
HIST_ROWS = 8
LANES = 128
VMEM_LIMIT_BYTES = 60 * 1024 * 1024

STAGE_ROWS = 1024
STAGE_COLS = 256
STAGE_SLOTS = 4
CAST_ROWS = 64

F32 = jnp.float32
BF16 = jnp.bfloat16


def _rmsnorm(x, g):
    return x * lax.rsqrt(jnp.mean(x * x, axis=-1, keepdims=True) + EPS) * g


def _silu(x):
    return x * jax.nn.sigmoid(x)


def _chunk_rows(k):
    return next(r for r in range(min(k, STAGE_ROWS), 0, -CAST_ROWS) if k % r == 0)


def _load_weights_as_bf16(srcs, dsts, stage_ref, sem_ref):
    tasks = []
    for src, dst in zip(srcs, dsts):
        k, n = src.shape
        rows = _chunk_rows(k)
        tasks += [(src, dst, r0, rows, c0)
                  for c0 in range(0, n, STAGE_COLS) for r0 in range(0, k, rows)]

    def copy(i):
        src, _, r0, rows, c0 = tasks[i]
        slot = i % STAGE_SLOTS
        return pltpu.make_async_copy(src.at[pl.ds(r0, rows), pl.ds(c0, STAGE_COLS)],
                                     stage_ref.at[slot, pl.ds(0, rows), :],
                                     sem_ref.at[slot])

    for i in range(min(STAGE_SLOTS, len(tasks))):
        copy(i).start()
    for i, (_, dst, r0, rows, c0) in enumerate(tasks):
        slot = i % STAGE_SLOTS
        copy(i).wait()

        def cast(j, carry, dst=dst, r0=r0, c0=c0, slot=slot):
            rr = pl.multiple_of(j * CAST_ROWS, CAST_ROWS)
            dst[pl.ds(r0 + rr, CAST_ROWS), c0:c0 + STAGE_COLS] = (
                stage_ref[slot, pl.ds(rr, CAST_ROWS), :].astype(BF16))
            return carry

        lax.fori_loop(0, rows // CAST_ROWS, cast, 0)
        if i + STAGE_SLOTS < len(tasks):
            copy(i + STAGE_SLOTS).start()


def _block_kernel(x_ref, xn_ref, p_ref, cq_ref, sq_ref, ck_ref, sk_ref,
                  mask_ref, qd_ref, kd_ref, cd_ref,
                  gmix_ref, win_hbm, convw_ref, gn_ref, wout_hbm,
                  gffn_ref, wg_hbm, wu_hbm, wd_hbm,
                  gple_ref, wpg_hbm, wpp_hbm, gfin_ref,
                  o_ref, state_ref, zbuf_ref, y_ref,
                  win_ref, wout_ref, wg_ref, wu_ref, wd_ref, wpg_ref, wpp_ref,
                  stage_ref, stage_sem, u0_ref):
    ts = SEQ_TILE
    s = pl.program_id(1)
    first_step = jnp.logical_and(pl.program_id(0) == 0, s == 0)

    @pl.when(first_step)
    def _():
        _load_weights_as_bf16(
            (win_hbm, wout_hbm, wg_hbm, wu_hbm, wd_hbm, wpg_hbm, wpp_hbm),
            (win_ref, wout_ref, wg_ref, wu_ref, wd_ref, wpg_ref, wpp_ref),
            stage_ref, stage_sem)
        u0_ref[...] = _rmsnorm(x_ref[0, 0:SUB_ROWS, :], gmix_ref[...]).astype(BF16)

    @pl.when(s == 0)
    def _():
        state_ref[...] = jnp.zeros_like(state_ref)
        zbuf_ref[:, 0:HIST_ROWS, :] = jnp.zeros((CONV_WIDTH // LANES, HIST_ROWS, LANES), F32)

    @pl.when(s > 0)
    def _():
        zbuf_ref[:, 0:HIST_ROWS, :] = zbuf_ref[:, ts:ts + HIST_ROWS, :]

    subs = [(i * SUB_ROWS, (i + 1) * SUB_ROWS) for i in range(ts // SUB_ROWS)]
    half = RET_HEAD_DIM // 2

    def in_proj(r0, r1):
        if r0 == 0:
            u = u0_ref[...]
        else:
            u = _rmsnorm(x_ref[0, r0:r1, :], gmix_ref[...]).astype(BF16)
        pc = jnp.dot(u, win_ref[:, 0:CONV_COLS], preferred_element_type=F32)
        pr = jnp.dot(u, win_ref[:, CONV_COLS:IN_COLS], preferred_element_type=F32)
        return pc, pr

    def conv_mixer(r0, r1, pc):
        for c in range(CONV_WIDTH // LANES):
            lo = c * LANES
            hi = lo + LANES
            z = pc[:, CONV_WIDTH + lo:CONV_WIDTH + hi] * pc[:, 2 * CONV_WIDTH + lo:2 * CONV_WIDTH + hi]
            zbuf_ref[c, HIST_ROWS + r0:HIST_ROWS + r1, :] = z
            z1 = zbuf_ref[c, HIST_ROWS - 1 + r0:HIST_ROWS - 1 + r1, :]
            z2 = zbuf_ref[c, HIST_ROWS - 2 + r0:HIST_ROWS - 2 + r1, :]
            conv = (convw_ref[0:1, lo:hi] * z2 + convw_ref[1:2, lo:hi] * z1
                    + convw_ref[2:3, lo:hi] * z)
            y_ref[r0:r1, lo:hi] = (pc[:, lo:hi] * conv).astype(BF16)

    def retention(r0, r1, pr, states):
        states = list(states)
        for c0 in range(0, r1 - r0, RET_CHUNK):
            c1 = c0 + RET_CHUNK
            t0, t1 = r0 + c0, r0 + c1
            scores, cross, vbs = [], [], []
            for h in range(RET_HEADS):
                lo = h * RET_HEAD_DIM
                hi = lo + RET_HEAD_DIM
                q = pr[c0:c1, lo:hi]
                k = pr[c0:c1, RET_WIDTH + lo:RET_WIDTH + hi]
                v = pr[c0:c1, 2 * RET_WIDTH + lo:2 * RET_WIDTH + hi]
                qr = q * cq_ref[t0:t1, :] + pltpu.roll(q, half, 1) * sq_ref[t0:t1, :]
                kr = k * ck_ref[t0:t1, :] + pltpu.roll(k, half, 1) * sk_ref[t0:t1, :]
                qb = qr.astype(BF16)
                kb = kr.astype(BF16)
                vb = v.astype(BF16)
                st = states[h]
                scores.append(lax.dot_general(qb, kb, (((1,), (1,)), ((), ())),
                                              preferred_element_type=F32))
                cross.append(jnp.dot((qr * qd_ref[h]).astype(BF16), st.astype(BF16),
                                     preferred_element_type=F32))
                kv = lax.dot_general((kr * kd_ref[h]).astype(BF16), vb,
                                     (((0,), (0,)), ((), ())), preferred_element_type=F32)
                states[h] = st * cd_ref[h] + kv
                vbs.append(vb)
            for h in range(RET_HEADS):
                lo = h * RET_HEAD_DIM
                hi = lo + RET_HEAD_DIM
                g = pr[c0:c1, 3 * RET_WIDTH + lo:3 * RET_WIDTH + hi]
                intra = jnp.dot((scores[h] * mask_ref[h]).astype(BF16), vbs[h],
                                preferred_element_type=F32)
                o = intra + cross[h]
                mu = jnp.mean(o, axis=-1, keepdims=True)
                d = o - mu
                var = jnp.mean(d * d, axis=-1, keepdims=True)
                yn = d * lax.rsqrt(var + EPS) * gn_ref[:, lo:hi]
                y_ref[t0:t1, CONV_WIDTH + lo:CONV_WIDTH + hi] = (_silu(g) * yn).astype(BF16)
        return states

    def out_proj(r0, r1):
        return x_ref[0, r0:r1, :] + jnp.dot(y_ref[r0:r1, :], wout_ref[...],
                                            preferred_element_type=F32)

    def ffn_hidden(h1):
        u2 = _rmsnorm(h1, gffn_ref[...]).astype(BF16)
        gate = jnp.dot(u2, wg_ref[...], preferred_element_type=F32)
        up = jnp.dot(u2, wu_ref[...], preferred_element_type=F32)
        return (_silu(gate) * up).astype(BF16)

    def ffn_down(h1, hid):
        return h1 + jnp.dot(hid, wd_ref[...], preferred_element_type=F32)

    def ple(r0, r1, h2):
        u3 = _rmsnorm(h2, gple_ref[...]).astype(BF16)
        pgate = jax.nn.sigmoid(jnp.dot(u3, wpg_ref[...], preferred_element_type=F32))
        pproj = jnp.dot(p_ref[0, r0:r1, :].astype(BF16), wpp_ref[...],
                        preferred_element_type=F32)
        return h2 + pgate * pproj

    proj = [in_proj(r0, r1) for r0, r1 in subs]
    states = [state_ref[h] for h in range(RET_HEADS)]
    for (r0, r1), (pc, pr) in zip(subs, proj):
        conv_mixer(r0, r1, pc)
        states = retention(r0, r1, pr, states)
    for h in range(RET_HEADS):
        state_ref[h] = states[h]
    h1 = [out_proj(r0, r1) for r0, r1 in subs]
    hid = [ffn_hidden(a) for a in h1]
    h2 = [ffn_down(a, b) for a, b in zip(h1, hid)]
    h3 = [ple(r0, r1, a) for (r0, r1), a in zip(subs, h2)]
    u0_ref[...] = _rmsnorm(xn_ref[0], gmix_ref[...]).astype(BF16)
    for (r0, r1), a in zip(subs, h3):
        o_ref[0, r0:r1, :] = _rmsnorm(a, gfin_ref[...])


def _rope_tables(seq, k_scale):
    half = RET_HEAD_DIM // 2
    pos = np.arange(seq, dtype=np.float64)
    inv_freq = ROPE_BASE ** (-np.arange(half, dtype=np.float64) / half)
    ang = pos[:, None] * inv_freq[None, :]
    cos_full = np.concatenate([np.cos(ang), np.cos(ang)], axis=-1)
    sin_full = np.concatenate([-np.sin(ang), np.sin(ang)], axis=-1)
    tables = (cos_full, sin_full, cos_full * k_scale, sin_full * k_scale)
    return tuple(t.astype(np.float32) for t in tables)


def _decay_tables(chunk):
    log_gamma = np.log(1.0 - np.power(2.0, -5.0 - np.arange(RET_HEADS, dtype=np.float64)))
    idx = np.arange(chunk, dtype=np.float64)
    diff = idx[:, None] - idx[None, :]
    mask = np.where(diff[None] >= 0,
                    np.exp(log_gamma[:, None, None] * np.maximum(diff, 0.0)[None]), 0.0)
    q_decay = np.exp(log_gamma[:, None] * (idx + 1.0)[None])
    k_decay = np.exp(log_gamma[:, None] * (chunk - 1 - idx)[None])
    chunk_decay = np.exp(log_gamma * chunk)
    lanes = (RET_HEADS, chunk, RET_HEAD_DIM)
    qd = np.broadcast_to(q_decay[:, :, None], lanes)
    kd = np.broadcast_to(k_decay[:, :, None], lanes)
    cd = np.broadcast_to(chunk_decay[:, None, None], (RET_HEADS, 1, RET_HEAD_DIM))
    return tuple(np.ascontiguousarray(t, dtype=np.float32) for t in (mask, qd, kd, cd))


def kernel(x, p, g_mix, w_in, conv_w, ret_gn, w_out, g_ffn, w_gate, w_up, w_down,
           g_ple, w_ple_gate, w_ple_proj, g_final):
    b, s, d = x.shape
    assert d == D_MODEL and s % SEQ_TILE == 0
    assert SEQ_TILE % SUB_ROWS == 0 and SUB_ROWS % RET_CHUNK == 0
    assert p.shape[0] == 1, "single-layer block"
    ts = SEQ_TILE

    cq, sq, ck, sk = _rope_tables(s, RET_HEAD_DIM ** -0.5)
    mask, qd, kd, cd = _decay_tables(RET_CHUNK)

    ns = s // ts
    subs_per_tile = ts // SUB_ROWS

    def next_first_sub(bi, si):
        nxt = jnp.minimum(bi * ns + si + 1, b * ns - 1)
        return nxt // ns, (nxt % ns) * subs_per_tile, 0

    row = lambda a: a.reshape(1, -1).astype(F32)
    whole = pl.BlockSpec(memory_space=pltpu.VMEM)
    in_hbm = pl.BlockSpec(memory_space=pl.ANY)
    weights = (w_in[0], w_out[0], w_gate[0], w_up[0], w_down[0], w_ple_gate[0], w_ple_proj[0])
    assert all(w.dtype == F32 and w.shape[1] % STAGE_COLS == 0 for w in weights)
    hw_in, hw_out, hw_gate, hw_up, hw_down, hw_pgate, hw_pproj = [(w, in_hbm) for w in weights]
    rope_spec = pl.BlockSpec((ts, RET_HEAD_DIM), lambda bi, si: (si, 0))

    operands = [
        (x, pl.BlockSpec((1, ts, D_MODEL), lambda bi, si: (bi, si, 0))),
        (x, pl.BlockSpec((1, SUB_ROWS, D_MODEL), next_first_sub)),
        (p[0], pl.BlockSpec((1, ts, PLE_DIM), lambda bi, si: (bi, si, 0))),
        (cq, rope_spec), (sq, rope_spec), (ck, rope_spec), (sk, rope_spec),
        (mask, whole), (qd, whole), (kd, whole), (cd, whole),
        (row(g_mix[0]), whole), hw_in,
        (conv_w[0].astype(F32), whole), (row(ret_gn[0]), whole), hw_out,
        (row(g_ffn[0]), whole), hw_gate, hw_up, hw_down,
        (row(g_ple[0]), whole), hw_pgate, hw_pproj, (row(g_final), whole),
    ]
    args = [a for a, _ in operands]
    in_specs = [sp for _, sp in operands]

    return pl.pallas_call(
        _block_kernel,
        grid=(b, s // ts),
        in_specs=in_specs,
        out_specs=pl.BlockSpec((1, ts, D_MODEL), lambda bi, si: (bi, si, 0)),
        out_shape=jax.ShapeDtypeStruct((b, s, D_MODEL), x.dtype),
        scratch_shapes=[
            pltpu.VMEM((RET_HEADS, RET_HEAD_DIM, RET_HEAD_DIM), F32),
            pltpu.VMEM((CONV_WIDTH // LANES, HIST_ROWS + ts, LANES), F32),
            pltpu.VMEM((ts, MIX_WIDTH), BF16),
            *[pltpu.VMEM(w.shape, BF16) for w in weights],
            pltpu.VMEM((STAGE_SLOTS, STAGE_ROWS, STAGE_COLS), F32),
            pltpu.SemaphoreType.DMA((STAGE_SLOTS,)),
            pltpu.VMEM((SUB_ROWS, D_MODEL), BF16),
        ],
        compiler_params=pltpu.CompilerParams(
            dimension_semantics=("arbitrary", "arbitrary"),
            vmem_limit_bytes=VMEM_LIMIT_BYTES),
        name="hybrid_block",
    )(*args)
```

```python
import jax
import jax.numpy as jnp
import numpy as np
from jax import lax
from jax.experimental import pallas as pl
from jax.experimental.pallas import tpu as pltpu

D_MODEL = 1024
PLE_DIM = 256
CONV_WIDTH = 512
CONV_K = 3
RET_HEADS = 4
RET_HEAD_DIM = 128
RET_WIDTH = RET_HEADS * RET_HEAD_DIM
MIX_WIDTH = CONV_WIDTH + RET_WIDTH
CONV_COLS = 3 * CONV_WIDTH
IN_COLS = CONV_COLS + 4 * RET_WIDTH
D_FF = 2816
ROPE_BASE = 10000.0
EPS = 1e-6

SEQ_TILE = 512
SUB_ROWS = 256
RET_CHUNK = 256
HIST_ROWS = 8
LANES = 128
VMEM_LIMIT_BYTES = 60 * 1024 * 1024

STAGE_ROWS = 1024
STAGE_COLS = 256
STAGE_SLOTS = 6
CAST_ROWS = 64

F32 = jnp.float32
BF16 = jnp.bfloat16


def _rmsnorm(x, g):
    return x * lax.rsqrt(jnp.mean(x * x, axis=-1, keepdims=True) + EPS) * g


def _silu(x):
    return x * jax.nn.sigmoid(x)


def _chunk_rows(k):
    return next(r for r in range(min(k, STAGE_ROWS), 0, -CAST_ROWS) if k % r == 0)


def _load_weights_as_bf16(srcs, dsts, stage_ref, sem_ref):
    tasks = []
    for src, dst in zip(srcs, dsts):
        k, n = src.shape
        rows = _chunk_rows(k)
        tasks += [(src, dst, r0, rows, c0)
                  for c0 in range(0, n, STAGE_COLS) for r0 in range(0, k, rows)]

    def copy(i):
        src, _, r0, rows, c0 = tasks[i]
        slot = i % STAGE_SLOTS
        return pltpu.make_async_copy(src.at[pl.ds(r0, rows), pl.ds(c0, STAGE_COLS)],
                                     stage_ref.at[slot, pl.ds(0, rows), :],
                                     sem_ref.at[slot])

    for i in range(min(STAGE_SLOTS, len(tasks))):
        copy(i).start()
    for i, (_, dst, r0, rows, c0) in enumerate(tasks):
        slot = i % STAGE_SLOTS
        copy(i).wait()

        def cast(j, carry, dst=dst, r0=r0, c0=c0, slot=slot):
            rr = pl.multiple_of(j * CAST_ROWS, CAST_ROWS)
            dst[pl.ds(r0 + rr, CAST_ROWS), c0:c0 + STAGE_COLS] = (
                stage_ref[slot, pl.ds(rr, CAST_ROWS), :].astype(BF16))
            return carry

        lax.fori_loop(0, rows // CAST_ROWS, cast, 0)
        if i + STAGE_SLOTS < len(tasks):
            copy(i + STAGE_SLOTS).start()


def _block_kernel(x_ref, p_ref, cos_ref, sin_ref,
                  mask_ref, qd_ref, kd_ref, cd_ref,
                  gmix_ref, win_hbm, convw_ref, gn_ref, wout_hbm,
                  gffn_ref, wg_hbm, wu_hbm, wd_hbm,
                  gple_ref, wpg_hbm, wpp_hbm, gfin_ref,
                  o_ref, state_ref, zbuf_ref, y_ref,
                  win_ref, wout_ref, wg_ref, wu_ref, wd_ref, wpg_ref, wpp_ref,
                  stage_ref, stage_sem):
    ts = SEQ_TILE
    s = pl.program_id(1)

    @pl.when(jnp.logical_and(pl.program_id(0) == 0, s == 0))
    def _():
        _load_weights_as_bf16(
            (win_hbm, wout_hbm, wg_hbm, wu_hbm, wd_hbm, wpg_hbm, wpp_hbm),
            (win_ref, wout_ref, wg_ref, wu_ref, wd_ref, wpg_ref, wpp_ref),
            stage_ref, stage_sem)

    @pl.when(s == 0)
    def _():
        state_ref[...] = jnp.zeros_like(state_ref)
        zbuf_ref[:, 0:HIST_ROWS, :] = jnp.zeros((CONV_WIDTH // LANES, HIST_ROWS, LANES), F32)

    @pl.when(s > 0)
    def _():
        zbuf_ref[:, 0:HIST_ROWS, :] = zbuf_ref[:, ts:ts + HIST_ROWS, :]

    subs = [(i * SUB_ROWS, (i + 1) * SUB_ROWS) for i in range(ts // SUB_ROWS)]
    half = RET_HEAD_DIM // 2

    def in_proj(r0, r1):
        x = x_ref[0, r0:r1, :]
        u = _rmsnorm(x, gmix_ref[...]).astype(BF16)
        pc = jnp.dot(u, win_ref[:, 0:CONV_COLS], preferred_element_type=F32)
        pr = jnp.dot(u, win_ref[:, CONV_COLS:IN_COLS], preferred_element_type=F32)
        return pc, pr

    def conv_mixer(r0, r1, pc):
        for c in range(CONV_WIDTH // LANES):
            lo = c * LANES
            hi = lo + LANES
            z = pc[:, CONV_WIDTH + lo:CONV_WIDTH + hi] * pc[:, 2 * CONV_WIDTH + lo:2 * CONV_WIDTH + hi]
            zbuf_ref[c, HIST_ROWS + r0:HIST_ROWS + r1, :] = z
            z1 = zbuf_ref[c, HIST_ROWS - 1 + r0:HIST_ROWS - 1 + r1, :]
            z2 = zbuf_ref[c, HIST_ROWS - 2 + r0:HIST_ROWS - 2 + r1, :]
            conv = (convw_ref[0:1, lo:hi] * z2 + convw_ref[1:2, lo:hi] * z1
                    + convw_ref[2:3, lo:hi] * z)
            y_ref[r0:r1, lo:hi] = (pc[:, lo:hi] * conv).astype(BF16)

    def retention(r0, r1, pr, states):
        states = list(states)
        for c0 in range(0, r1 - r0, RET_CHUNK):
            c1 = c0 + RET_CHUNK
            t0, t1 = r0 + c0, r0 + c1
            pos0 = pl.multiple_of(s * ts + t0, RET_CHUNK)
            cos = cos_ref[pl.ds(pos0, RET_CHUNK), :]
            sin = sin_ref[pl.ds(pos0, RET_CHUNK), :]
            scores, cross, vbs = [], [], []
            for h in range(RET_HEADS):
                lo = h * RET_HEAD_DIM
                hi = lo + RET_HEAD_DIM
                q = pr[c0:c1, lo:hi]
                k = pr[c0:c1, RET_WIDTH + lo:RET_WIDTH + hi]
                v = pr[c0:c1, 2 * RET_WIDTH + lo:2 * RET_WIDTH + hi]
                qr = q * cos + pltpu.roll(q, half, 1) * sin
                kr = k * cos + pltpu.roll(k, half, 1) * sin
                qb = qr.astype(BF16)
                kb = kr.astype(BF16)
                vb = v.astype(BF16)
                st = states[h]
                scores.append(lax.dot_general(qb, kb, (((1,), (1,)), ((), ())),
                                              preferred_element_type=F32))
                cross.append(jnp.dot((qr * qd_ref[h]).astype(BF16), st.astype(BF16),
                                     preferred_element_type=F32))
                kv = lax.dot_general((kr * kd_ref[h]).astype(BF16), vb,
                                     (((0,), (0,)), ((), ())), preferred_element_type=F32)
                states[h] = st * cd_ref[h] + kv
                vbs.append(vb)
            for h in range(RET_HEADS):
                lo = h * RET_HEAD_DIM
                hi = lo + RET_HEAD_DIM
                g = pr[c0:c1, 3 * RET_WIDTH + lo:3 * RET_WIDTH + hi]
                intra = jnp.dot((scores[h] * mask_ref[h]).astype(BF16), vbs[h],
                                preferred_element_type=F32)
                o = intra + cross[h]
                mu = jnp.mean(o, axis=-1, keepdims=True)
                d = o - mu
                var = jnp.mean(d * d, axis=-1, keepdims=True)
                yn = d * lax.rsqrt(var + EPS) * gn_ref[:, lo:hi]
                y_ref[t0:t1, CONV_WIDTH + lo:CONV_WIDTH + hi] = (_silu(g) * yn).astype(BF16)
        return states

    def out_proj(r0, r1):
        return x_ref[0, r0:r1, :] + jnp.dot(y_ref[r0:r1, :], wout_ref[...],
                                            preferred_element_type=F32)

    def ffn_hidden(h1):
        u2 = _rmsnorm(h1, gffn_ref[...]).astype(BF16)
        gate = jnp.dot(u2, wg_ref[...], preferred_element_type=F32)
        up = jnp.dot(u2, wu_ref[...], preferred_element_type=F32)
        return (_silu(gate) * up).astype(BF16)

    def ffn_down(h1, hid):
        return h1 + jnp.dot(hid, wd_ref[...], preferred_element_type=F32)

    def ple(r0, r1, h2):
        u3 = _rmsnorm(h2, gple_ref[...]).astype(BF16)
        pgate = jax.nn.sigmoid(jnp.dot(u3, wpg_ref[...], preferred_element_type=F32))
        pproj = jnp.dot(p_ref[0, r0:r1, :].astype(BF16), wpp_ref[...],
                        preferred_element_type=F32)
        return h2 + pgate * pproj

    proj = [in_proj(r0, r1) for r0, r1 in subs]
    states = [state_ref[h] for h in range(RET_HEADS)]
    for (r0, r1), (pc, pr) in zip(subs, proj):
        conv_mixer(r0, r1, pc)
        states = retention(r0, r1, pr, states)
    for h in range(RET_HEADS):
        state_ref[h] = states[h]
    h1 = [out_proj(r0, r1) for r0, r1 in subs]
    hid = [ffn_hidden(a) for a in h1]
    h2 = [ffn_down(a, b) for a, b in zip(h1, hid)]
    h3 = [ple(r0, r1, a) for (r0, r1), a in zip(subs, h2)]
    for (r0, r1), a in zip(subs, h3):
        o_ref[0, r0:r1, :] = _rmsnorm(a, gfin_ref[...])


def _rope_tables(seq):
    half = RET_HEAD_DIM // 2
    pos = np.arange(seq, dtype=np.float64)
    inv_freq = ROPE_BASE ** (-np.arange(half, dtype=np.float64) / half)
    ang = pos[:, None] * inv_freq[None, :]
    cos_full = np.concatenate([np.cos(ang), np.cos(ang)], axis=-1)
    sin_full = np.concatenate([-np.sin(ang), np.sin(ang)], axis=-1)
    return cos_full.astype(np.float32), sin_full.astype(np.float32)


def _decay_tables(chunk, k_scale):
    log_gamma = np.log(1.0 - np.power(2.0, -5.0 - np.arange(RET_HEADS, dtype=np.float64)))
    idx = np.arange(chunk, dtype=np.float64)
    diff = idx[:, None] - idx[None, :]
    mask = np.where(diff[None] >= 0,
                    np.exp(log_gamma[:, None, None] * np.maximum(diff, 0.0)[None]), 0.0)
    q_decay = np.exp(log_gamma[:, None] * (idx + 1.0)[None])
    k_decay = np.exp(log_gamma[:, None] * (chunk - 1 - idx)[None])
    chunk_decay = np.exp(log_gamma * chunk)
    lanes = (RET_HEADS, chunk, RET_HEAD_DIM)
    qd = np.broadcast_to(q_decay[:, :, None], lanes)
    kd = np.broadcast_to(k_decay[:, :, None], lanes)
    cd = np.broadcast_to(chunk_decay[:, None, None], (RET_HEADS, 1, RET_HEAD_DIM))
    tables = (mask * k_scale, qd, kd * k_scale, cd)
    return tuple(np.ascontiguousarray(t, dtype=np.float32) for t in tables)


def kernel(x, p, g_mix, w_in, conv_w, ret_gn, w_out, g_ffn, w_gate, w_up, w_down,
           g_ple, w_ple_gate, w_ple_proj, g_final):
    b, s, d = x.shape
    assert d == D_MODEL and s % SEQ_TILE == 0
    assert SEQ_TILE % SUB_ROWS == 0 and SUB_ROWS % RET_CHUNK == 0
    assert p.shape[0] == 1, "single-layer block"
    ts = SEQ_TILE

    cos, sin = _rope_tables(s)
    mask, qd, kd, cd = _decay_tables(RET_CHUNK, RET_HEAD_DIM ** -0.5)

    row = lambda a: a.reshape(1, -1).astype(F32)
    whole = pl.BlockSpec(memory_space=pltpu.VMEM)
    in_hbm = pl.BlockSpec(memory_space=pl.ANY)
    weights = (w_in[0], w_out[0], w_gate[0], w_up[0], w_down[0], w_ple_gate[0], w_ple_proj[0])
    assert all(w.dtype == F32 and w.shape[1] % STAGE_COLS == 0 for w in weights)
    hw_in, hw_out, hw_gate, hw_up, hw_down, hw_pgate, hw_pproj = [(w, in_hbm) for w in weights]

    operands = [
        (x, pl.BlockSpec((1, ts, D_MODEL), lambda bi, si: (bi, si, 0))),
        (p[0], pl.BlockSpec((1, ts, PLE_DIM), lambda bi, si: (bi, si, 0))),
        (cos, whole), (sin, whole),
        (mask, whole), (qd, whole), (kd, whole), (cd, whole),
        (row(g_mix[0]), whole), hw_in,
        (conv_w[0].astype(F32), whole), (row(ret_gn[0]), whole), hw_out,
        (row(g_ffn[0]), whole), hw_gate, hw_up, hw_down,
        (row(g_ple[0]), whole), hw_pgate, hw_pproj, (row(g_final), whole),
    ]
    args = [a for a, _ in operands]
    in_specs = [sp for _, sp in operands]

    return pl.pallas_call(
        _block_kernel,
        grid=(b, s // ts),
        in_specs=in_specs,
        out_specs=pl.BlockSpec((1, ts, D_MODEL), lambda bi, si: (bi, si, 0)),
        out_shape=jax.ShapeDtypeStruct((b, s, D_MODEL), x.dtype),
        scratch_shapes=[
            pltpu.VMEM((RET_HEADS, RET_HEAD_DIM, RET_HEAD_DIM), F32),
            pltpu.VMEM((CONV_WIDTH // LANES, HIST_ROWS + ts, LANES), F32),
            pltpu.VMEM((ts, MIX_WIDTH), BF16),
            *[pltpu.VMEM(w.shape, BF16) for w in weights],
            pltpu.VMEM((STAGE_SLOTS, STAGE_ROWS, STAGE_COLS), F32),
            pltpu.SemaphoreType.DMA((STAGE_SLOTS,)),
        ],
        compiler_params=pltpu.CompilerParams(
            dimension_semantics=("arbitrary", "arbitrary"),
            vmem_limit_bytes=VMEM_LIMIT_BYTES),
        name="hybrid_block",
    )(*args)
```
